```python
import jax, jax.numpy as jnp
from jax import lax
import numpy as np

D_MODEL = 2048
BATCH = 4
SEQ = 2048
DEPTH = 2

N_A_LAYERS = DEPTH // 2
N_B_LAYERS = DEPTH - N_A_LAYERS
N_DENSE_LAYERS = (DEPTH + 1) // 2
N_MOE_LAYERS = DEPTH // 2

CONV_WIDTH = 31
FFN_DIM = 5632
N_EXPERTS = 8
TOP_K = 2
EXPERT_DIM = 7168
N_HEADS = D_MODEL // 128
QK_NOPE_DIM = 128
QK_ROPE_DIM = 64
V_DIM = 128
Q_LORA_RANK = 512
KV_LORA_RANK = 512
ROPE_THETA = 10000.0
Q_BLOCK = 128
NORM_EPS = 1e-6

kernel_name = "yoco_conformer_mla_moe_trunk"


def rmsnorm(x, g):
    xf = x.astype(jnp.float32)
    y = xf * lax.rsqrt(jnp.mean(xf * xf, axis=-1, keepdims=True) + NORM_EPS)
    return (y * g.astype(jnp.float32)).astype(x.dtype)


def layernorm(x, g, b):
    xf = x.astype(jnp.float32)
    mu = jnp.mean(xf, axis=-1, keepdims=True)
    xc = xf - mu
    y = xc * lax.rsqrt(jnp.mean(xc * xc, axis=-1, keepdims=True) + NORM_EPS)
    return (y * g.astype(jnp.float32) + b.astype(jnp.float32)).astype(x.dtype)


def rope_tables(seq):
    pos = jnp.arange(seq, dtype=jnp.float32)
    inv_freq = ROPE_THETA ** (-jnp.arange(0, QK_ROPE_DIM, 2, dtype=jnp.float32) / QK_ROPE_DIM)
    ang = pos[:, None] * inv_freq[None, :]
    return jnp.cos(ang), jnp.sin(ang)


def apply_rope(x, cos, sin):
    shape = (cos.shape[0],) + (1,) * (x.ndim - 3) + (cos.shape[1],)
    c = cos.reshape(shape).astype(x.dtype)
    s = sin.reshape(shape).astype(x.dtype)
    x1, x2 = jnp.split(x, 2, axis=-1)
    return jnp.concatenate([x1 * c - x2 * s, x2 * c + x1 * s], axis=-1)


def conv_module(h, w_pw1, b_pw1, w_dw, b_dw, ln_g, ln_b, w_pw2):
    a, g = jnp.split(h @ w_pw1 + b_pw1, 2, axis=-1)
    u = a * jax.nn.sigmoid(g)
    u = lax.conv_general_dilated(
        u, w_dw[:, None, :].astype(u.dtype), window_strides=(1,),
        padding=((CONV_WIDTH - 1, 0),),
        dimension_numbers=("NWC", "WIO", "NWC"),
        feature_group_count=D_MODEL) + b_dw
    u = jax.nn.silu(layernorm(u, ln_g, ln_b))
    return u @ w_pw2


def dense_swiglu(h, w_gu, w_down):
    g, u = jnp.split(h @ w_gu, 2, axis=-1)
    return (jax.nn.silu(g) * u) @ w_down


def moe_swiglu(h, w_router, w_gu, w_down):
    b, s, d = h.shape
    t = h.reshape(b * s, d)
    logits = (t @ w_router).astype(jnp.float32)
    top_logits, top_idx = lax.top_k(logits, TOP_K)
    top_w = jax.nn.softmax(top_logits, axis=-1)
    combine = jnp.sum(jax.nn.one_hot(top_idx, N_EXPERTS, dtype=jnp.float32)
                      * top_w[..., None], axis=1).astype(h.dtype)
    out = jnp.zeros_like(t)
    for e in range(N_EXPERTS):
        g, u = jnp.split(t @ w_gu[e], 2, axis=-1)
        out = out + combine[:, e:e + 1] * ((jax.nn.silu(g) * u) @ w_down[e])
    return out.reshape(b, s, d)


def shared_kv(x, kv_in_g, w_kv_a, kv_latent_g, w_kv_b, cos, sin):
    b, s, _ = x.shape
    h = rmsnorm(x, kv_in_g)
    c_kv, k_rope = jnp.split(h @ w_kv_a, [KV_LORA_RANK], axis=-1)
    c_kv = rmsnorm(c_kv, kv_latent_g)
    k_rope = apply_rope(k_rope, cos, sin)
    kv = (c_kv @ w_kv_b).reshape(b, s, N_HEADS, QK_NOPE_DIM + V_DIM)
    k_nope, v = jnp.split(kv, [QK_NOPE_DIM], axis=-1)
    return k_nope, k_rope, v


def mla_attention(h, w_q_a, q_latent_g, w_q_b, w_o, k_nope, k_rope, v, cos, sin):
    b, s, _ = h.shape
    q = (rmsnorm(h @ w_q_a, q_latent_g) @ w_q_b).reshape(b, s, N_HEADS, QK_NOPE_DIM + QK_ROPE_DIM)
    q_nope, q_rope = jnp.split(q, [QK_NOPE_DIM], axis=-1)
    q_rope = apply_rope(q_rope, cos, sin)
    scale = (QK_NOPE_DIM + QK_ROPE_DIM) ** -0.5
    n_blocks = s // Q_BLOCK
    qn = q_nope.reshape(b, n_blocks, Q_BLOCK, N_HEADS, QK_NOPE_DIM).transpose(1, 0, 2, 3, 4)
    qr = q_rope.reshape(b, n_blocks, Q_BLOCK, N_HEADS, QK_ROPE_DIM).transpose(1, 0, 2, 3, 4)
    k_pos = jnp.arange(s)

    def block(args):
        qn_b, qr_b, i = args
        sc = (jnp.einsum("bqhd,bkhd->bhqk", qn_b, k_nope)
              + jnp.einsum("bqhr,bkr->bhqk", qr_b, k_rope)).astype(jnp.float32) * scale
        q_pos = i * Q_BLOCK + jnp.arange(Q_BLOCK)
        sc = jnp.where(q_pos[:, None] >= k_pos[None, :], sc, -jnp.inf)
        p = jax.nn.softmax(sc, axis=-1).astype(v.dtype)
        return jnp.einsum("bhqk,bkhd->bqhd", p, v)

    o = lax.map(block, (qn, qr, jnp.arange(n_blocks)))
    o = o.transpose(1, 0, 2, 3, 4).reshape(b, s, N_HEADS * V_DIM)
    return o @ w_o


def setup_inputs(seed: int = 0) -> dict:
    key = jax.random.key(seed)
    ks = jax.random.split(key, 24)
    f32 = jnp.float32

    def w(k, shape, fan_in):
        return jax.random.normal(k, shape, f32) * (fan_in ** -0.5)

    def gain(k, shape):
        return 1.0 + 0.02 * jax.random.normal(k, shape, f32)

    def bias(k, shape):
        return 0.02 * jax.random.normal(k, shape, f32)

    D = D_MODEL
    return {
        "x": jax.random.normal(ks[0], (BATCH, SEQ, D), f32),
        "norm_g": gain(ks[1], (DEPTH, 4, D)),
        "conv_w_pw1": w(ks[2], (N_A_LAYERS, D, 2 * D), D),
        "conv_b_pw1": bias(ks[3], (N_A_LAYERS, 2 * D)),
        "conv_w_dw": w(ks[4], (N_A_LAYERS, CONV_WIDTH, D), CONV_WIDTH),
        "conv_b_dw": bias(ks[5], (N_A_LAYERS, D)),
        "conv_ln_g": gain(ks[6], (N_A_LAYERS, D)),
        "conv_ln_b": bias(ks[7], (N_A_LAYERS, D)),
        "conv_w_pw2": w(ks[8], (N_A_LAYERS, D, D), D),
        "ffn_w_gu": w(ks[9], (N_DENSE_LAYERS, D, 2 * FFN_DIM), D),
        "ffn_w_down": w(ks[10], (N_DENSE_LAYERS, FFN_DIM, D), FFN_DIM),
        "moe_w_router": w(ks[11], (N_MOE_LAYERS, D, N_EXPERTS), D),
        "moe_w_gu": w(ks[12], (N_MOE_LAYERS, N_EXPERTS, D, 2 * EXPERT_DIM), D),
        "moe_w_down": w(ks[13], (N_MOE_LAYERS, N_EXPERTS, EXPERT_DIM, D), EXPERT_DIM),
        "kv_in_g": gain(ks[14], (D,)),
        "kv_w_a": w(ks[15], (D, KV_LORA_RANK + QK_ROPE_DIM), D),
        "kv_latent_g": gain(ks[16], (KV_LORA_RANK,)),
        "kv_w_b": w(ks[17], (KV_LORA_RANK, N_HEADS * (QK_NOPE_DIM + V_DIM)), KV_LORA_RANK),
        "attn_w_q_a": w(ks[18], (N_B_LAYERS, D, Q_LORA_RANK), D),
        "attn_q_latent_g": gain(ks[19], (N_B_LAYERS, Q_LORA_RANK)),
        "attn_w_q_b": w(ks[20], (N_B_LAYERS, Q_LORA_RANK, N_HEADS * (QK_NOPE_DIM + QK_ROPE_DIM)), Q_LORA_RANK),
        "attn_w_o": w(ks[21], (N_B_LAYERS, N_HEADS * V_DIM, D), N_HEADS * V_DIM),
    }


def reference(x, norm_g, conv_w_pw1, conv_b_pw1, conv_w_dw, conv_b_dw, conv_ln_g, conv_ln_b,
              conv_w_pw2, ffn_w_gu, ffn_w_down, moe_w_router, moe_w_gu, moe_w_down,
              kv_in_g, kv_w_a, kv_latent_g, kv_w_b, attn_w_q_a, attn_q_latent_g,
              attn_w_q_b, attn_w_o):
    cos, sin = rope_tables(x.shape[1])
    for i in range(DEPTH):
        if i == N_A_LAYERS:
            k_nope, k_rope, v = shared_kv(x, kv_in_g, kv_w_a, kv_latent_g, kv_w_b, cos, sin)
        h = rmsnorm(x, norm_g[i, 0])
        if i < N_A_LAYERS:
            m = conv_module(h, conv_w_pw1[i], conv_b_pw1[i], conv_w_dw[i], conv_b_dw[i],
                            conv_ln_g[i], conv_ln_b[i], conv_w_pw2[i])
        else:
            j = i - N_A_LAYERS
            m = mla_attention(h, attn_w_q_a[j], attn_q_latent_g[j], attn_w_q_b[j], attn_w_o[j],
                              k_nope, k_rope, v, cos, sin)
        x = x + rmsnorm(m, norm_g[i, 1])
        h = rmsnorm(x, norm_g[i, 2])
        if i % 2 == 0:
            f = dense_swiglu(h, ffn_w_gu[i // 2], ffn_w_down[i // 2])
        else:
            f = moe_swiglu(h, moe_w_router[i // 2], moe_w_gu[i // 2], moe_w_down[i // 2])
        x = x + rmsnorm(f, norm_g[i, 3])
    return x
```

```python
import functools

import jax
import jax.numpy as jnp
from jax import lax
from jax.experimental import pallas as pl
from jax.experimental.pallas import tpu as pltpu

NORM_EPS = 1e-6
ROPE_THETA = 10000.0
QK_NOPE_DIM = 128
QK_ROPE_DIM = 64
V_DIM = 128
TOP_K = 2

LANES = 128
CONV_HALO = 32
VMEM_LIMIT = 56 * 1024 * 1024

F32 = jnp.float32
BF16 = jnp.bfloat16


def _params(semantics):
    return pltpu.CompilerParams(dimension_semantics=semantics, vmem_limit_bytes=VMEM_LIMIT)


def _rms(x):
    return x * lax.rsqrt(jnp.mean(x * x, axis=-1, keepdims=True) + NORM_EPS)


def _pw1_glu_kernel(x_ref, g_ref, wa_ref, wg_ref, ba_ref, bg_ref, o_ref, xn_ref):
    @pl.when(pl.program_id(1) == 0)
    def _():
        xn_ref[...] = (_rms(x_ref[...]) * g_ref[...]).astype(BF16)

    xn = xn_ref[...]
    a = jnp.dot(xn, wa_ref[...], preferred_element_type=F32) + ba_ref[...]
    g = jnp.dot(xn, wg_ref[...], preferred_element_type=F32) + bg_ref[...]
    o_ref[...] = (a * jax.nn.sigmoid(g)).astype(o_ref.dtype)


def _pw1_glu(x, g, w, b, tm, tn):
    n, d = x.shape
    nj = d // tn
    return pl.pallas_call(
        _pw1_glu_kernel,
        grid=(n // tm, nj),
        in_specs=[
            pl.BlockSpec((tm, d), lambda i, j: (i, 0)),
            pl.BlockSpec((1, d), lambda i, j: (0, 0)),
            pl.BlockSpec((d, tn), lambda i, j: (0, j)),
            pl.BlockSpec((d, tn), lambda i, j: (0, nj + j)),
            pl.BlockSpec((1, tn), lambda i, j: (0, j)),
            pl.BlockSpec((1, tn), lambda i, j: (0, nj + j)),
        ],
        out_specs=pl.BlockSpec((tm, tn), lambda i, j: (i, j)),
        out_shape=jax.ShapeDtypeStruct((n, d), BF16),
        scratch_shapes=[pltpu.VMEM((tm, d), BF16)],
        compiler_params=_params(("parallel", "arbitrary")),
        name="conv_pw1_glu",
    )(x, g, w, w, b, b)


def _dwconv_kernel(u_ref, halo_ref, w_ref, b_ref, o_ref, ext_ref, *, tiles_per_seq, width, rb):
    tm = u_ref.shape[0]
    first = (pl.program_id(0) % tiles_per_seq) == 0
    halo = halo_ref[...].astype(F32)
    ext_ref[0:CONV_HALO, :] = jnp.where(first, jnp.zeros_like(halo), halo)
    ext_ref[CONV_HALO:, :] = u_ref[...].astype(F32)
    base = CONV_HALO - (width - 1)
    for r0 in range(0, tm, rb):
        acc = w_ref[0:1, :] * ext_ref[base + r0:base + r0 + rb, :]
        for k in range(1, width):
            acc = acc + w_ref[k:k + 1, :] * ext_ref[base + r0 + k:base + r0 + k + rb, :]
        o_ref[r0:r0 + rb, :] = (acc + b_ref[...]).astype(o_ref.dtype)


def _dwconv(u, w, b, seq, tm, tc):
    n, d = u.shape
    width = w.shape[0]
    hb = tm // CONV_HALO
    kern = functools.partial(_dwconv_kernel, tiles_per_seq=seq // tm, width=width, rb=min(64, tm))
    return pl.pallas_call(
        kern,
        grid=(n // tm, d // tc),
        in_specs=[
            pl.BlockSpec((tm, tc), lambda i, j: (i, j)),
            pl.BlockSpec((CONV_HALO, tc), lambda i, j: (jnp.maximum(i * hb - 1, 0), j)),
            pl.BlockSpec((width, tc), lambda i, j: (0, j)),
            pl.BlockSpec((1, tc), lambda i, j: (0, j)),
        ],
        out_specs=pl.BlockSpec((tm, tc), lambda i, j: (i, j)),
        out_shape=jax.ShapeDtypeStruct((n, d), BF16),
        scratch_shapes=[pltpu.VMEM((tm + CONV_HALO, tc), F32)],
        compiler_params=_params(("parallel", "parallel")),
        name="conv_depthwise",
    )(u, u, w, b)


def _ln_pw2_kernel(c_ref, x_ref, lg_ref, lb_ref, w_ref, g_ref, o_ref):
    c = c_ref[...].astype(F32)
    mu = jnp.mean(c, axis=-1, keepdims=True)
    cc = c - mu
    y = cc * lax.rsqrt(jnp.mean(cc * cc, axis=-1, keepdims=True) + NORM_EPS)
    y = y * lg_ref[...] + lb_ref[...]
    y = (y * jax.nn.sigmoid(y)).astype(BF16)
    m = jnp.dot(y, w_ref[...], preferred_element_type=F32)
    o_ref[...] = x_ref[...] + _rms(m) * g_ref[...]


def _ln_pw2(c, x, ln_g, ln_b, w, g, tm):
    n, d = x.shape
    row = pl.BlockSpec((tm, d), lambda i: (i, 0))
    vec = pl.BlockSpec((1, d), lambda i: (0, 0))
    return pl.pallas_call(
        _ln_pw2_kernel,
        grid=(n // tm,),
        in_specs=[row, row, vec, vec, pl.BlockSpec((d, d), lambda i: (0, 0)), vec],
        out_specs=row,
        out_shape=jax.ShapeDtypeStruct((n, d), F32),
        compiler_params=_params(("parallel",)),
        name="conv_ln_pw2",
    )(c, x, ln_g, ln_b, w, g)


def _swiglu_chunk(xn_ref, wg_ref, wu_ref, wd_ref, acc_ref, c):
    xn = xn_ref[...]
    g = jnp.dot(xn, wg_ref[...].astype(BF16), preferred_element_type=F32)
    u = jnp.dot(xn, wu_ref[...].astype(BF16), preferred_element_type=F32)
    hid = (g * jax.nn.sigmoid(g) * u).astype(BF16)
    part = jnp.dot(hid, wd_ref[...].astype(BF16), preferred_element_type=F32)

    @pl.when(c == 0)
    def _():
        acc_ref[...] = part

    @pl.when(c > 0)
    def _():
        acc_ref[...] += part


def _dense_ffn_kernel(x_ref, gi_ref, go_ref, wg_ref, wu_ref, wd_ref, o_ref, xn_ref, acc_ref):
    c = pl.program_id(1)

    @pl.when(c == 0)
    def _():
        xn_ref[...] = (_rms(x_ref[...]) * gi_ref[...]).astype(BF16)

    _swiglu_chunk(xn_ref, wg_ref, wu_ref, wd_ref, acc_ref, c)

    @pl.when(c == pl.num_programs(1) - 1)
    def _():
        o_ref[...] = x_ref[...] + _rms(acc_ref[...]) * go_ref[...]


def _dense_ffn(x, g_in, g_out, w_gu, w_down, tm, tc):
    n, d = x.shape
    f = w_down.shape[0]
    nc = f // tc
    row = pl.BlockSpec((tm, d), lambda i, c: (i, 0))
    vec = pl.BlockSpec((1, d), lambda i, c: (0, 0))
    return pl.pallas_call(
        _dense_ffn_kernel,
        grid=(n // tm, nc),
        in_specs=[
            row, vec, vec,
            pl.BlockSpec((d, tc), lambda i, c: (0, c)),
            pl.BlockSpec((d, tc), lambda i, c: (0, nc + c)),
            pl.BlockSpec((tc, d), lambda i, c: (c, 0)),
        ],
        out_specs=row,
        out_shape=jax.ShapeDtypeStruct((n, d), F32),
        scratch_shapes=[pltpu.VMEM((tm, d), BF16), pltpu.VMEM((tm, d), F32)],
        compiler_params=_params(("parallel", "arbitrary")),
        name="dense_swiglu",
    )(x, g_in, g_out, w_gu, w_gu, w_down)


def _rope128(t, cos2, sin2):
    return t * cos2 + pltpu.roll(t, 64, axis=1) * sin2


def _kv_proj_kernel(x_ref, g_ref, wa_ref, lg_ref, wb_ref, cos_ref, sin_ref, k_ref, v_ref,
                    *, heads, rank):
    h = (_rms(x_ref[...]) * g_ref[...]).astype(BF16)
    a = jnp.dot(h, wa_ref[...], preferred_element_type=F32)
    c_kv = (_rms(a[:, :rank]) * lg_ref[...]).astype(BF16)
    k_rope = _rope128(a[:, rank:], cos_ref[...], sin_ref[...]).astype(BF16)
    kv = jnp.dot(c_kv, wb_ref[...], preferred_element_type=F32)
    hd = heads * QK_NOPE_DIM
    for hh in range(heads):
        k_ref[:, hh * 256:hh * 256 + 128] = kv[:, hh * 128:(hh + 1) * 128].astype(BF16)
        k_ref[:, hh * 256 + 128:(hh + 1) * 256] = k_rope
    v_ref[...] = kv[:, hd:].astype(BF16)


def _kv_proj(x, g, wa_ext, lat_g, wb_split, cos2, sin2, seq, tm):
    n, d = x.shape
    heads = d // 128
    rank = lat_g.shape[1]
    spt = seq // tm
    kern = functools.partial(_kv_proj_kernel, heads=heads, rank=rank)
    full = lambda a: pl.BlockSpec(a.shape, lambda i: (0, 0))
    return pl.pallas_call(
        kern,
        grid=(n // tm,),
        in_specs=[
            pl.BlockSpec((tm, d), lambda i: (i, 0)), full(g), full(wa_ext), full(lat_g),
            full(wb_split),
            pl.BlockSpec((tm, LANES), lambda i: (i % spt, 0)),
            pl.BlockSpec((tm, LANES), lambda i: (i % spt, 0)),
        ],
        out_specs=[pl.BlockSpec((tm, heads * 256), lambda i: (i, 0)),
                   pl.BlockSpec((tm, heads * 128), lambda i: (i, 0))],
        out_shape=[jax.ShapeDtypeStruct((n, heads * 256), BF16),
                   jax.ShapeDtypeStruct((n, heads * 128), BF16)],
        compiler_params=_params(("parallel",)),
        name="mla_kv_proj",
    )(x, g, wa_ext, lat_g, wb_split, cos2, sin2)


def _q_proj_kernel(x_ref, g_ref, wa_ref, lg_ref, wb_ref, cos_ref, sin_ref, q_ref, *, heads, scale):
    h = (_rms(x_ref[...]) * g_ref[...]).astype(BF16)
    lat = jnp.dot(h, wa_ref[...], preferred_element_type=F32)
    lat = (_rms(lat) * lg_ref[...]).astype(BF16)
    q = jnp.dot(lat, wb_ref[...], preferred_element_type=F32)
    cos2 = cos_ref[...]
    sin2 = sin_ref[...]
    for hh in range(heads):
        q_ref[:, hh * 256:hh * 256 + 128] = (q[:, hh * 256:hh * 256 + 128] * scale).astype(BF16)
        r = _rope128(q[:, hh * 256 + 128:(hh + 1) * 256], cos2, sin2)
        q_ref[:, hh * 256 + 128:(hh + 1) * 256] = (r * scale).astype(BF16)


def _q_proj(x, g, wa, lat_g, wb_ext, cos2, sin2, seq, tm):
    n, d = x.shape
    heads = d // 128
    spt = seq // tm
    scale = float(QK_NOPE_DIM + QK_ROPE_DIM) ** -0.5
    kern = functools.partial(_q_proj_kernel, heads=heads, scale=scale)
    full = lambda a: pl.BlockSpec(a.shape, lambda i: (0, 0))
    return pl.pallas_call(
        kern,
        grid=(n // tm,),
        in_specs=[
            pl.BlockSpec((tm, d), lambda i: (i, 0)), full(g), full(wa), full(lat_g), full(wb_ext),
            pl.BlockSpec((tm, LANES), lambda i: (i % spt, 0)),
            pl.BlockSpec((tm, LANES), lambda i: (i % spt, 0)),
        ],
        out_specs=pl.BlockSpec((tm, heads * 256), lambda i: (i, 0)),
        out_shape=jax.ShapeDtypeStruct((n, heads * 256), BF16),
        compiler_params=_params(("parallel",)),
        name="mla_q_proj",
    )(x, g, wa, lat_g, wb_ext, cos2, sin2)


def _attn_kernel(q_ref, k_ref, v_ref, o_ref, *, tk):
    tq = q_ref.shape[0]
    qi = pl.program_id(2)
    q = q_ref[...]
    row = qi * tq + lax.broadcasted_iota(jnp.int32, (tq, tk), 0)
    col0 = lax.broadcasted_iota(jnp.int32, (tq, tk), 1)

    def body(j, carry):
        m, l, acc = carry
        start = pl.multiple_of(j * tk, tk)
        k = k_ref[pl.ds(start, tk), :]
        s = lax.dot_general(q, k, (((1,), (1,)), ((), ())), preferred_element_type=F32)
        s = jnp.where(row >= col0 + start, s, -jnp.inf)
        m_new = jnp.maximum(m, jnp.max(s, axis=1, keepdims=True))
        p = jnp.exp(s - m_new)
        alpha = jnp.exp(m - m_new)
        l = alpha * l + jnp.sum(p, axis=1, keepdims=True)
        pv = jnp.dot(p.astype(BF16), v_ref[pl.ds(start, tk), :], preferred_element_type=F32)
        return m_new, l, alpha * acc + pv

    init = (jnp.full((tq, 1), -jnp.inf, F32), jnp.zeros((tq, 1), F32),
            jnp.zeros((tq, V_DIM), F32))
    n_kv = (qi + 1) * (tq // tk)
    _, l, acc = lax.fori_loop(0, n_kv, body, init)
    o_ref[...] = (acc / l).astype(o_ref.dtype)


def _attention(q, k, v, batch, seq, tq, tk):
    n = q.shape[0]
    heads = v.shape[1] // V_DIM
    nq = seq // tq
    kern = functools.partial(_attn_kernel, tk=tk)
    return pl.pallas_call(
        kern,
        grid=(batch, heads, nq),
        in_specs=[
            pl.BlockSpec((tq, 256), lambda b, h, i: (b * nq + i, h)),
            pl.BlockSpec((seq, 256), lambda b, h, i: (b, h)),
            pl.BlockSpec((seq, V_DIM), lambda b, h, i: (b, h)),
        ],
        out_specs=pl.BlockSpec((tq, V_DIM), lambda b, h, i: (b * nq + i, h)),
        out_shape=jax.ShapeDtypeStruct((n, heads * V_DIM), BF16),
        compiler_params=_params(("parallel", "parallel", "arbitrary")),
        name="mla_attention",
    )(q, k, v)


def _attn_out_kernel(o_ref, x_ref, w_ref, g_ref, xo_ref):
    m = jnp.dot(o_ref[...], w_ref[...], preferred_element_type=F32)
    xo_ref[...] = x_ref[...] + _rms(m) * g_ref[...]


def _attn_out(o, x, w, g, tm):
    n, d = x.shape
    row = pl.BlockSpec((tm, d), lambda i: (i, 0))
    return pl.pallas_call(
        _attn_out_kernel,
        grid=(n // tm,),
        in_specs=[pl.BlockSpec((tm, o.shape[1]), lambda i: (i, 0)), row,
                  pl.BlockSpec(w.shape, lambda i: (0, 0)), pl.BlockSpec((1, d), lambda i: (0, 0))],
        out_specs=row,
        out_shape=jax.ShapeDtypeStruct((n, d), F32),
        compiler_params=_params(("parallel",)),
        name="mla_out_proj",
    )(o, x, w, g)


def _router_kernel(x_ref, g_ref, whi_ref, wlo_ref, h_ref, ri_ref, rw_ref, cnt_ref, base_ref,
                   *, experts, slabs):
    tm = x_ref.shape[0]

    @pl.when(pl.program_id(0) == 0)
    def _():
        base_ref[...] = jnp.zeros_like(base_ref)

    h = _rms(x_ref[...]) * g_ref[...]
    for f in range(slabs):
        h_ref[pl.ds(f, tm, stride=slabs), :] = h[:, f * LANES:(f + 1) * LANES]

    h_hi = h.astype(BF16)
    h_lo = (h - h_hi.astype(F32)).astype(BF16)
    logits = (jnp.dot(h_hi, whi_ref[...], preferred_element_type=F32)
              + jnp.dot(h_lo, whi_ref[...], preferred_element_type=F32)
              + jnp.dot(h_hi, wlo_ref[...], preferred_element_type=F32))

    lane = lax.broadcasted_iota(jnp.int32, (tm, LANES), 1)
    lg = jnp.where(lane < experts, logits, -jnp.inf)
    m1 = jnp.max(lg, axis=1, keepdims=True)
    i1 = jnp.min(jnp.where(lg == m1, lane, LANES), axis=1, keepdims=True)
    oh1 = lane == i1
    lg2 = jnp.where(oh1, -jnp.inf, lg)
    m2 = jnp.max(lg2, axis=1, keepdims=True)
    i2 = jnp.min(jnp.where(lg2 == m2, lane, LANES), axis=1, keepdims=True)
    oh2 = lane == i2
    e2 = jnp.exp(m2 - m1)
    w1 = 1.0 / (1.0 + e2)
    w2 = e2 / (1.0 + e2)

    cnt = oh1.astype(F32) + oh2.astype(F32)
    r_i = lax.broadcasted_iota(jnp.int32, (tm, tm), 0)
    c_i = lax.broadcasted_iota(jnp.int32, (tm, tm), 1)
    tri = (c_i < r_i).astype(BF16)
    before = jnp.dot(tri, cnt.astype(BF16), preferred_element_type=F32) + base_ref[0:1, :]
    rank1 = jnp.sum(jnp.where(oh1, before, 0.0), axis=1, keepdims=True).astype(jnp.int32)
    rank2 = jnp.sum(jnp.where(oh2, before, 0.0), axis=1, keepdims=True).astype(jnp.int32)
    total = base_ref[0:1, :] + jnp.sum(cnt, axis=0, keepdims=True)
    base_ref[...] = jnp.broadcast_to(total, base_ref.shape)
    cnt_ref[...] = jnp.broadcast_to(total, cnt_ref.shape).astype(jnp.int32)

    ri_ref[...] = jnp.where(lane == 0, i1, jnp.where(lane == 1, i2,
                            jnp.where(lane == 2, rank1, jnp.where(lane == 3, rank2, 0))))
    rw_ref[...] = jnp.where(lane == 0, w1, jnp.where(lane == 1, w2, 0.0))


def _router(x, g, w_router, tm):
    n, d = x.shape
    experts = w_router.shape[1]
    slabs = d // LANES
    w_pad = jnp.zeros((d, LANES), F32).at[:, :experts].set(w_router)
    w_hi = w_pad.astype(BF16)
    w_lo = (w_pad - w_hi.astype(F32)).astype(BF16)
    kern = functools.partial(_router_kernel, experts=experts, slabs=slabs)
    wspec = pl.BlockSpec((d, LANES), lambda i: (0, 0))
    return pl.pallas_call(
        kern,
        grid=(n // tm,),
        in_specs=[pl.BlockSpec((tm, d), lambda i: (i, 0)), pl.BlockSpec((1, d), lambda i: (0, 0)),
                  wspec, wspec],
        out_specs=[pl.BlockSpec((tm * slabs, LANES), lambda i: (i, 0)),
                   pl.BlockSpec((tm, LANES), lambda i: (i, 0)),
                   pl.BlockSpec((tm, LANES), lambda i: (i, 0)),
                   pl.BlockSpec((8, LANES), lambda i: (0, 0))],
        out_shape=[jax.ShapeDtypeStruct((n * slabs, LANES), F32),
                   jax.ShapeDtypeStruct((n, LANES), jnp.int32),
                   jax.ShapeDtypeStruct((n, LANES), F32),
                   jax.ShapeDtypeStruct((8, LANES), jnp.int32)],
        scratch_shapes=[pltpu.VMEM((8, LANES), F32)],
        compiler_params=_params(("arbitrary",)),
        name="moe_router",
    )(x, g, w_hi, w_lo)


def _row_copy(src_ref, dst_ref, sem, src_row, dst_row, slabs):
    return pltpu.make_async_copy(
        src_ref.at[pl.ds(pl.multiple_of(src_row * slabs, slabs), slabs), :],
        dst_ref.at[pl.ds(pl.multiple_of(dst_row * slabs, slabs), slabs), :],
        sem)


def _expert_ffn_kernel(te_ref, tv_ref, tok_ref, h_ref, wg_ref, wu_ref, wd_ref, y_ref,
                       gbuf_ref, xn_ref, acc_ref, sem, *, slabs):
    i = pl.program_id(0)
    c = pl.program_id(1)
    tm = xn_ref.shape[0]
    live = tv_ref[i] == 1

    @pl.when(jnp.logical_and(live, c == 0))
    def _():
        def start(r, carry):
            _row_copy(h_ref, gbuf_ref, sem, tok_ref[i * tm + r], r, slabs).start()
            return carry

        lax.fori_loop(0, tm, start, 0)

        def wait(r, carry):
            _row_copy(h_ref, gbuf_ref, sem, tok_ref[i * tm + r], r, slabs).wait()
            return carry

        lax.fori_loop(0, tm, wait, 0)
        for f in range(slabs):
            xn_ref[:, f * LANES:(f + 1) * LANES] = (
                gbuf_ref[pl.ds(f, tm, stride=slabs), :].astype(BF16))

    @pl.when(live)
    def _():
        _swiglu_chunk(xn_ref, wg_ref, wu_ref, wd_ref, acc_ref, c)

    @pl.when(c == pl.num_programs(1) - 1)
    def _():
        @pl.when(live)
        def _():
            y_ref[...] = acc_ref[...]

        @pl.when(jnp.logical_not(live))
        def _():
            y_ref[...] = jnp.zeros_like(y_ref)


def _expert_ffn(h_slabs, tile_expert, tile_live, tok_of_slot, w_gu, w_down, layer, d, tm, tc):
    rows = tok_of_slot.shape[0]
    fe = w_down.shape[2]
    nc = fe // tc
    slabs = d // LANES
    last = nc - 1

    def chunk(c, tv, i):
        return c * tv[i] + last * (1 - tv[i])

    grid_spec = pltpu.PrefetchScalarGridSpec(
        num_scalar_prefetch=3,
        grid=(rows // tm, nc),
        in_specs=[
            pl.BlockSpec(memory_space=pl.ANY),
            pl.BlockSpec((None, None, d, tc),
                         lambda i, c, te, tv, tok: (layer, te[i], 0, chunk(c, tv, i))),
            pl.BlockSpec((None, None, d, tc),
                         lambda i, c, te, tv, tok: (layer, te[i], 0, nc + chunk(c, tv, i))),
            pl.BlockSpec((None, None, tc, d),
                         lambda i, c, te, tv, tok: (layer, te[i], chunk(c, tv, i), 0)),
        ],
        out_specs=pl.BlockSpec((tm, d), lambda i, c, te, tv, tok: (i, 0)),
        scratch_shapes=[pltpu.VMEM((tm * slabs, LANES), F32), pltpu.VMEM((tm, d), BF16),
                        pltpu.VMEM((tm, d), F32), pltpu.SemaphoreType.DMA(())],
    )
    return pl.pallas_call(
        functools.partial(_expert_ffn_kernel, slabs=slabs),
        grid_spec=grid_spec,
        out_shape=jax.ShapeDtypeStruct((rows, d), F32),
        compiler_params=_params(("arbitrary", "arbitrary")),
        name="moe_expert_swiglu",
    )(tile_expert, tile_live, tok_of_slot, h_slabs, w_gu, w_gu, w_down)


def _combine_kernel(s1_ref, s2_ref, y_ref, rw_ref, x_ref, g_ref, o_ref, b1_ref, b2_ref, sem,
                    *, slabs):
    i = pl.program_id(0)
    tm = x_ref.shape[0]

    def start(r, carry):
        _row_copy(y_ref, b1_ref, sem.at[0], s1_ref[i * tm + r], r, slabs).start()
        _row_copy(y_ref, b2_ref, sem.at[1], s2_ref[i * tm + r], r, slabs).start()
        return carry

    lax.fori_loop(0, tm, start, 0)

    def wait(r, carry):
        _row_copy(y_ref, b1_ref, sem.at[0], s1_ref[i * tm + r], r, slabs).wait()
        _row_copy(y_ref, b2_ref, sem.at[1], s2_ref[i * tm + r], r, slabs).wait()
        return carry

    lax.fori_loop(0, tm, wait, 0)

    rw = rw_ref[...]
    w1 = rw[:, 0:1]
    w2 = rw[:, 1:2]
    parts = []
    ssq = jnp.zeros((tm, 1), F32)
    for f in range(slabs):
        m = (w1 * b1_ref[pl.ds(f, tm, stride=slabs), :]
             + w2 * b2_ref[pl.ds(f, tm, stride=slabs), :])
        ssq = ssq + jnp.sum(m * m, axis=1, keepdims=True)
        parts.append(m)
    inv = lax.rsqrt(ssq / (slabs * LANES) + NORM_EPS)
    for f in range(slabs):
        sl = slice(f * LANES, (f + 1) * LANES)
        o_ref[:, sl] = x_ref[:, sl] + parts[f] * inv * g_ref[:, sl]


def _combine(slot1, slot2, y_slabs, rw, x, g, tm):
    n, d = x.shape
    slabs = d // LANES
    grid_spec = pltpu.PrefetchScalarGridSpec(
        num_scalar_prefetch=2,
        grid=(n // tm,),
        in_specs=[
            pl.BlockSpec(memory_space=pl.ANY),
            pl.BlockSpec((tm, LANES), lambda i, s1, s2: (i, 0)),
            pl.BlockSpec((tm, d), lambda i, s1, s2: (i, 0)),
            pl.BlockSpec((1, d), lambda i, s1, s2: (0, 0)),
        ],
        out_specs=pl.BlockSpec((tm, d), lambda i, s1, s2: (i, 0)),
        scratch_shapes=[pltpu.VMEM((tm * slabs, LANES), F32), pltpu.VMEM((tm * slabs, LANES), F32),
                        pltpu.SemaphoreType.DMA((2,))],
    )
    return pl.pallas_call(
        functools.partial(_combine_kernel, slabs=slabs),
        grid_spec=grid_spec,
        out_shape=jax.ShapeDtypeStruct((n, d), F32),
        compiler_params=_params(("arbitrary",)),
        name="moe_combine",
    )(slot1, slot2, y_slabs, rw, x, g)


def _moe_ffn(x, g_in, g_out, w_router, w_gu, w_down, layer, tiles):
    n, d = x.shape
    experts = w_router.shape[1]
    slabs = d // LANES
    tm = tiles["expert_rows"]
    h_slabs, ri, rw, cnt = _router(x, g_in, w_router, tiles["router_rows"])

    counts = cnt[0, :experts]
    padded = ((counts + tm - 1) // tm) * tm
    ends = jnp.cumsum(padded)
    starts = ends - padded
    slot1 = starts[ri[:, 0]] + ri[:, 2]
    slot2 = starts[ri[:, 1]] + ri[:, 3]
    rows = TOP_K * n + experts * tm
    token = jnp.arange(n, dtype=jnp.int32)
    tok_of_slot = jnp.zeros((rows,), jnp.int32).at[slot1].set(token).at[slot2].set(token)
    tile_start = jnp.arange(rows // tm, dtype=jnp.int32) * tm
    tile_live = (tile_start < ends[-1]).astype(jnp.int32)
    tile_expert = jnp.minimum(jnp.searchsorted(ends, tile_start, side="right"),
                              experts - 1).astype(jnp.int32)

    y = _expert_ffn(h_slabs, tile_expert, tile_live, tok_of_slot, w_gu, w_down, layer, d, tm,
                    tiles["expert_chunk"])
    y_slabs = y.reshape(rows * slabs, LANES)
    return _combine(slot1.astype(jnp.int32), slot2.astype(jnp.int32), y_slabs, rw, x, g_out,
                    tiles["combine_rows"])


def _tiles(n, seq, d, ffn, expert_ffn):
    pick = lambda pref, dim: min(pref, dim)
    return {
        "pw1_rows": pick(512, n), "pw1_cols": pick(512, d),
        "conv_rows": pick(512, seq), "conv_cols": pick(256, d),
        "ln_rows": pick(256, n),
        "ffn_rows": pick(512, n), "ffn_chunk": pick(512, ffn),
        "proj_rows": pick(256, seq),
        "attn_q": pick(512, seq), "attn_k": pick(512, seq),
        "out_rows": pick(256, n),
        "router_rows": pick(512, n),
        "expert_rows": pick(512, n), "expert_chunk": pick(256, expert_ffn),
        "combine_rows": pick(256, n),
    }


def _rope_tables(seq):
    pos = jnp.arange(seq, dtype=F32)
    inv_freq = ROPE_THETA ** (-jnp.arange(0, QK_ROPE_DIM, 2, dtype=F32) / QK_ROPE_DIM)
    ang = pos[:, None] * inv_freq[None, :]
    cos, sin = jnp.cos(ang), jnp.sin(ang)
    zeros = jnp.zeros((seq, LANES - QK_ROPE_DIM), F32)
    return (jnp.concatenate([cos, cos, zeros], axis=1),
            jnp.concatenate([-sin, sin, zeros], axis=1))


def _swap_halves(w):
    half = w.shape[-1] // 2
    return jnp.concatenate([w[..., half:], w[..., :half]], axis=-1)


def kernel(x, norm_g, conv_w_pw1, conv_b_pw1, conv_w_dw, conv_b_dw, conv_ln_g, conv_ln_b,
           conv_w_pw2, ffn_w_gu, ffn_w_down, moe_w_router, moe_w_gu, moe_w_down,
           kv_in_g, kv_w_a, kv_latent_g, kv_w_b, attn_w_q_a, attn_q_latent_g,
           attn_w_q_b, attn_w_o):
    batch, seq, d = x.shape
    n = batch * seq
    depth = norm_g.shape[0]
    n_conv = depth // 2
    heads = d // 128
    kv_rank = kv_latent_g.shape[0]
    tiles = _tiles(n, seq, d, ffn_w_down.shape[1], moe_w_down.shape[2])
    cos2, sin2 = _rope_tables(seq)
    vec = lambda v: v.reshape(1, -1)

    xs = x.reshape(n, d)
    k = v = None
    for i in range(depth):
        if i == n_conv:
            rope_w = kv_w_a[:, kv_rank:]
            wa_ext = jnp.concatenate([kv_w_a, _swap_halves(rope_w)], axis=1).astype(BF16)
            wb = kv_w_b.reshape(kv_rank, heads, QK_NOPE_DIM + V_DIM)
            wb_split = jnp.concatenate(
                [wb[:, :, :QK_NOPE_DIM].reshape(kv_rank, heads * QK_NOPE_DIM),
                 wb[:, :, QK_NOPE_DIM:].reshape(kv_rank, heads * V_DIM)], axis=1).astype(BF16)
            k, v = _kv_proj(xs, vec(kv_in_g), wa_ext, vec(kv_latent_g), wb_split, cos2, sin2,
                            seq, tiles["proj_rows"])
        if i < n_conv:
            u = _pw1_glu(xs, vec(norm_g[i, 0]), conv_w_pw1[i].astype(BF16), vec(conv_b_pw1[i]),
                         tiles["pw1_rows"], tiles["pw1_cols"])
            c = _dwconv(u, conv_w_dw[i], vec(conv_b_dw[i]), seq, tiles["conv_rows"],
                        tiles["conv_cols"])
            xs = _ln_pw2(c, xs, vec(conv_ln_g[i]), vec(conv_ln_b[i]), conv_w_pw2[i].astype(BF16),
                         vec(norm_g[i, 1]), tiles["ln_rows"])
        else:
            j = i - n_conv
            q_rank = attn_w_q_a.shape[2]
            wqb = attn_w_q_b[j].reshape(q_rank, heads, QK_NOPE_DIM + QK_ROPE_DIM)
            rope_w = wqb[:, :, QK_NOPE_DIM:]
            wqb_ext = jnp.concatenate([wqb, _swap_halves(rope_w)], axis=2)
            wqb_ext = wqb_ext.reshape(q_rank, heads * 256).astype(BF16)
            q = _q_proj(xs, vec(norm_g[i, 0]), attn_w_q_a[j].astype(BF16),
                        vec(attn_q_latent_g[j]), wqb_ext, cos2, sin2, seq, tiles["proj_rows"])
            o = _attention(q, k, v, batch, seq, tiles["attn_q"], tiles["attn_k"])
            xs = _attn_out(o, xs, attn_w_o[j].astype(BF16), vec(norm_g[i, 1]), tiles["out_rows"])
        if i % 2 == 0:
            xs = _dense_ffn(xs, vec(norm_g[i, 2]), vec(norm_g[i, 3]), ffn_w_gu[i // 2].astype(BF16),
                            ffn_w_down[i // 2].astype(BF16), tiles["ffn_rows"], tiles["ffn_chunk"])
        else:
            xs = _moe_ffn(xs, vec(norm_g[i, 2]), vec(norm_g[i, 3]), moe_w_router[i // 2],
                          moe_w_gu, moe_w_down, i // 2, tiles)
    return xs.reshape(batch, seq, d)
```

```python
import functools

import jax
import jax.numpy as jnp
from jax import lax
from jax.experimental import pallas as pl
from jax.experimental.pallas import tpu as pltpu

NORM_EPS = 1e-6
ROPE_THETA = 10000.0
QK_NOPE_DIM = 128
QK_ROPE_DIM = 64
V_DIM = 128
TOP_K = 2

LANES = 128
SUBLANES = 8
CONV_HALO = 32
VMEM_LIMIT = 56 * 1024 * 1024

F32 = jnp.float32
BF16 = jnp.bfloat16


def _params(semantics):
    return pltpu.CompilerParams(dimension_semantics=semantics, vmem_limit_bytes=VMEM_LIMIT)


def _rms(x):
    return x * lax.rsqrt(jnp.mean(x * x, axis=-1, keepdims=True) + NORM_EPS)


def _pw1_glu_kernel(x_ref, g_ref, wa_ref, wg_ref, ba_ref, bg_ref, o_ref, xn_ref):
    @pl.when(pl.program_id(1) == 0)
    def _():
        xn_ref[...] = (_rms(x_ref[...]) * g_ref[...]).astype(BF16)

    xn = xn_ref[...]
    a = jnp.dot(xn, wa_ref[...], preferred_element_type=F32) + ba_ref[...]
    g = jnp.dot(xn, wg_ref[...], preferred_element_type=F32) + bg_ref[...]
    o_ref[...] = (a * jax.nn.sigmoid(g)).astype(o_ref.dtype)


def _pw1_glu(x, g, w, b, tm, tn):
    n, d = x.shape
    nj = d // tn
    return pl.pallas_call(
        _pw1_glu_kernel,
        grid=(n // tm, nj),
        in_specs=[
            pl.BlockSpec((tm, d), lambda i, j: (i, 0)),
            pl.BlockSpec((1, d), lambda i, j: (0, 0)),
            pl.BlockSpec((d, tn), lambda i, j: (0, j)),
            pl.BlockSpec((d, tn), lambda i, j: (0, nj + j)),
            pl.BlockSpec((1, tn), lambda i, j: (0, j)),
            pl.BlockSpec((1, tn), lambda i, j: (0, nj + j)),
        ],
        out_specs=pl.BlockSpec((tm, tn), lambda i, j: (i, j)),
        out_shape=jax.ShapeDtypeStruct((n, d), BF16),
        scratch_shapes=[pltpu.VMEM((tm, d), BF16)],
        compiler_params=_params(("parallel", "arbitrary")),
        name="conv_pw1_glu",
    )(x, g, w, w, b, b)


def _dwconv_kernel(u_ref, halo_ref, w_ref, b_ref, o_ref, sh_ref, *, tiles_per_seq, width, rb):
    tm = u_ref.shape[0]
    ext = tm + CONV_HALO
    first = (pl.program_id(0) % tiles_per_seq) == 0
    halo = halo_ref[...].astype(F32)
    sh_ref[0, 0:CONV_HALO, :] = jnp.where(first, jnp.zeros_like(halo), halo)
    sh_ref[0, CONV_HALO:ext, :] = u_ref[...].astype(F32)
    for j in range(1, SUBLANES):
        sh_ref[j, 0:ext - SUBLANES, :] = sh_ref[0, j:ext - SUBLANES + j, :]
    base = CONV_HALO - (width - 1)

    def block(blk, carry):
        r0 = pl.multiple_of(blk * rb, rb)
        acc = None
        for k in range(width):
            j = (base + k) % SUBLANES
            a = (base + k) - j
            term = w_ref[k:k + 1, :] * sh_ref[j, pl.ds(r0 + a, rb), :]
            acc = term if acc is None else acc + term
        o_ref[pl.ds(r0, rb), :] = (acc + b_ref[...]).astype(o_ref.dtype)
        return carry

    lax.fori_loop(0, tm // rb, block, 0)


def _dwconv(u, w, b, seq, tm, tc):
    n, d = u.shape
    width = w.shape[0]
    hb = tm // CONV_HALO
    kern = functools.partial(_dwconv_kernel, tiles_per_seq=seq // tm, width=width, rb=min(64, tm))
    return pl.pallas_call(
        kern,
        grid=(n // tm, d // tc),
        in_specs=[
            pl.BlockSpec((tm, tc), lambda i, j: (i, j)),
            pl.BlockSpec((CONV_HALO, tc), lambda i, j: (jnp.maximum(i * hb - 1, 0), j)),
            pl.BlockSpec((width, tc), lambda i, j: (0, j)),
            pl.BlockSpec((1, tc), lambda i, j: (0, j)),
        ],
        out_specs=pl.BlockSpec((tm, tc), lambda i, j: (i, j)),
        out_shape=jax.ShapeDtypeStruct((n, d), BF16),
        scratch_shapes=[pltpu.VMEM((SUBLANES, tm + CONV_HALO, tc), F32)],
        compiler_params=_params(("parallel", "parallel")),
        name="conv_depthwise",
    )(u, u, w, b)


def _ln_pw2_kernel(c_ref, x_ref, lg_ref, lb_ref, w_ref, g_ref, o_ref):
    c = c_ref[...].astype(F32)
    mu = jnp.mean(c, axis=-1, keepdims=True)
    cc = c - mu
    y = cc * lax.rsqrt(jnp.mean(cc * cc, axis=-1, keepdims=True) + NORM_EPS)
    y = y * lg_ref[...] + lb_ref[...]
    y = (y * jax.nn.sigmoid(y)).astype(BF16)
    m = jnp.dot(y, w_ref[...], preferred_element_type=F32)
    o_ref[...] = x_ref[...] + _rms(m) * g_ref[...]


def _ln_pw2(c, x, ln_g, ln_b, w, g, tm):
    n, d = x.shape
    row = pl.BlockSpec((tm, d), lambda i: (i, 0))
    vec = pl.BlockSpec((1, d), lambda i: (0, 0))
    return pl.pallas_call(
        _ln_pw2_kernel,
        grid=(n // tm,),
        in_specs=[row, row, vec, vec, pl.BlockSpec((d, d), lambda i: (0, 0)), vec],
        out_specs=row,
        out_shape=jax.ShapeDtypeStruct((n, d), F32),
        compiler_params=_params(("parallel",)),
        name="conv_ln_pw2",
    )(c, x, ln_g, ln_b, w, g)


def _swiglu_chunk(xn_ref, wg_ref, wu_ref, wd_ref, acc_ref, c):
    xn = xn_ref[...]
    g = jnp.dot(xn, wg_ref[...].astype(BF16), preferred_element_type=F32)
    u = jnp.dot(xn, wu_ref[...].astype(BF16), preferred_element_type=F32)
    hid = (g * jax.nn.sigmoid(g) * u).astype(BF16)
    part = jnp.dot(hid, wd_ref[...].astype(BF16), preferred_element_type=F32)

    @pl.when(c == 0)
    def _():
        acc_ref[...] = part

    @pl.when(c > 0)
    def _():
        acc_ref[...] += part


def _dense_ffn_kernel(x_ref, gi_ref, go_ref, wg_ref, wu_ref, wd_ref, o_ref, xn_ref, acc_ref):
    c = pl.program_id(1)

    @pl.when(c == 0)
    def _():
        xn_ref[...] = (_rms(x_ref[...]) * gi_ref[...]).astype(BF16)

    _swiglu_chunk(xn_ref, wg_ref, wu_ref, wd_ref, acc_ref, c)

    @pl.when(c == pl.num_programs(1) - 1)
    def _():
        o_ref[...] = x_ref[...] + _rms(acc_ref[...]) * go_ref[...]


def _dense_ffn(x, g_in, g_out, w_gu, w_down, tm, tc):
    n, d = x.shape
    f = w_down.shape[0]
    nc = f // tc
    row = pl.BlockSpec((tm, d), lambda i, c: (i, 0))
    vec = pl.BlockSpec((1, d), lambda i, c: (0, 0))
    return pl.pallas_call(
        _dense_ffn_kernel,
        grid=(n // tm, nc),
        in_specs=[
            row, vec, vec,
            pl.BlockSpec((d, tc), lambda i, c: (0, c)),
            pl.BlockSpec((d, tc), lambda i, c: (0, nc + c)),
            pl.BlockSpec((tc, d), lambda i, c: (c, 0)),
        ],
        out_specs=row,
        out_shape=jax.ShapeDtypeStruct((n, d), F32),
        scratch_shapes=[pltpu.VMEM((tm, d), BF16), pltpu.VMEM((tm, d), F32)],
        compiler_params=_params(("parallel", "arbitrary")),
        name="dense_swiglu",
    )(x, g_in, g_out, w_gu, w_gu, w_down)


def _rope128(t, cos2, sin2):
    return t * cos2 + pltpu.roll(t, 64, axis=1) * sin2


def _kv_proj_kernel(x_ref, g_ref, wa_ref, lg_ref, wb_ref, cos_ref, sin_ref, k_ref, v_ref,
                    *, heads, rank):
    h = (_rms(x_ref[...]) * g_ref[...]).astype(BF16)
    a = jnp.dot(h, wa_ref[...], preferred_element_type=F32)
    c_kv = (_rms(a[:, :rank]) * lg_ref[...]).astype(BF16)
    k_rope = _rope128(a[:, rank:], cos_ref[...], sin_ref[...]).astype(BF16)
    kv = jnp.dot(c_kv, wb_ref[...], preferred_element_type=F32)
    hd = heads * QK_NOPE_DIM
    for hh in range(heads):
        k_ref[:, hh * 256:hh * 256 + 128] = kv[:, hh * 128:(hh + 1) * 128].astype(BF16)
        k_ref[:, hh * 256 + 128:(hh + 1) * 256] = k_rope
    v_ref[...] = kv[:, hd:].astype(BF16)


def _kv_proj(x, g, wa_ext, lat_g, wb_split, cos2, sin2, seq, tm):
    n, d = x.shape
    heads = d // 128
    rank = lat_g.shape[1]
    spt = seq // tm
    kern = functools.partial(_kv_proj_kernel, heads=heads, rank=rank)
    full = lambda a: pl.BlockSpec(a.shape, lambda i: (0, 0))
    return pl.pallas_call(
        kern,
        grid=(n // tm,),
        in_specs=[
            pl.BlockSpec((tm, d), lambda i: (i, 0)), full(g), full(wa_ext), full(lat_g),
            full(wb_split),
            pl.BlockSpec((tm, LANES), lambda i: (i % spt, 0)),
            pl.BlockSpec((tm, LANES), lambda i: (i % spt, 0)),
        ],
        out_specs=[pl.BlockSpec((tm, heads * 256), lambda i: (i, 0)),
                   pl.BlockSpec((tm, heads * 128), lambda i: (i, 0))],
        out_shape=[jax.ShapeDtypeStruct((n, heads * 256), BF16),
                   jax.ShapeDtypeStruct((n, heads * 128), BF16)],
        compiler_params=_params(("parallel",)),
        name="mla_kv_proj",
    )(x, g, wa_ext, lat_g, wb_split, cos2, sin2)


def _q_proj_kernel(x_ref, g_ref, wa_ref, lg_ref, wb_ref, cos_ref, sin_ref, q_ref, *, heads, scale):
    h = (_rms(x_ref[...]) * g_ref[...]).astype(BF16)
    lat = jnp.dot(h, wa_ref[...], preferred_element_type=F32)
    lat = (_rms(lat) * lg_ref[...]).astype(BF16)
    q = jnp.dot(lat, wb_ref[...], preferred_element_type=F32)
    cos2 = cos_ref[...]
    sin2 = sin_ref[...]
    for hh in range(heads):
        q_ref[:, hh * 256:hh * 256 + 128] = (q[:, hh * 256:hh * 256 + 128] * scale).astype(BF16)
        r = _rope128(q[:, hh * 256 + 128:(hh + 1) * 256], cos2, sin2)
        q_ref[:, hh * 256 + 128:(hh + 1) * 256] = (r * scale).astype(BF16)


def _q_proj(x, g, wa, lat_g, wb_ext, cos2, sin2, seq, tm):
    n, d = x.shape
    heads = d // 128
    spt = seq // tm
    scale = float(QK_NOPE_DIM + QK_ROPE_DIM) ** -0.5
    kern = functools.partial(_q_proj_kernel, heads=heads, scale=scale)
    full = lambda a: pl.BlockSpec(a.shape, lambda i: (0, 0))
    return pl.pallas_call(
        kern,
        grid=(n // tm,),
        in_specs=[
            pl.BlockSpec((tm, d), lambda i: (i, 0)), full(g), full(wa), full(lat_g), full(wb_ext),
            pl.BlockSpec((tm, LANES), lambda i: (i % spt, 0)),
            pl.BlockSpec((tm, LANES), lambda i: (i % spt, 0)),
        ],
        out_specs=pl.BlockSpec((tm, heads * 256), lambda i: (i, 0)),
        out_shape=jax.ShapeDtypeStruct((n, heads * 256), BF16),
        compiler_params=_params(("parallel",)),
        name="mla_q_proj",
    )(x, g, wa, lat_g, wb_ext, cos2, sin2)


def _attn_kernel(q_ref, k_ref, v_ref, o_ref, *, tk, parts):
    tq = q_ref.shape[0]
    rows = tq // parts
    qi = pl.program_id(2)
    qs = [q_ref[g * rows:(g + 1) * rows, :] for g in range(parts)]
    row = lax.broadcasted_iota(jnp.int32, (rows, tk), 0)
    col = lax.broadcasted_iota(jnp.int32, (rows, tk), 1)

    def tile(j, carry, masked):
        start = pl.multiple_of(j * tk, tk)
        k = k_ref[pl.ds(start, tk), :]
        v = v_ref[pl.ds(start, tk), :]
        out = []
        for g in range(parts):
            m, l, acc = carry[g]
            s = lax.dot_general(qs[g], k, (((1,), (1,)), ((), ())), preferred_element_type=F32)
            if masked:
                s = jnp.where(row + (qi * tq + g * rows) >= col + start, s, -jnp.inf)
            m_new = jnp.maximum(m, jnp.max(s, axis=1, keepdims=True))
            p = jnp.exp(s - m_new)
            alpha = jnp.exp(m - m_new)
            l = alpha * l + jnp.sum(p, axis=1, keepdims=True)
            pv = jnp.dot(p.astype(BF16), v, preferred_element_type=F32)
            out.append((m_new, l, alpha * acc + pv))
        return tuple(out)

    init = tuple((jnp.full((rows, 1), -jnp.inf, F32), jnp.zeros((rows, 1), F32),
                  jnp.zeros((rows, V_DIM), F32)) for _ in range(parts))
    n_full = qi * (tq // tk)
    carry = lax.fori_loop(0, n_full, lambda j, c: tile(j, c, False), init)
    for d in range(tq // tk):
        carry = tile(n_full + d, carry, True)
    for g in range(parts):
        _, l, acc = carry[g]
        o_ref[g * rows:(g + 1) * rows, :] = (acc / l).astype(o_ref.dtype)


def _attention(q, k, v, batch, seq, tq, tk):
    n = q.shape[0]
    heads = v.shape[1] // V_DIM
    nq = seq // tq
    kern = functools.partial(_attn_kernel, tk=tk, parts=1)
    return pl.pallas_call(
        kern,
        grid=(batch, heads, nq),
        in_specs=[
            pl.BlockSpec((tq, 256), lambda b, h, i: (b * nq + i, h)),
            pl.BlockSpec((seq, 256), lambda b, h, i: (b, h)),
            pl.BlockSpec((seq, V_DIM), lambda b, h, i: (b, h)),
        ],
        out_specs=pl.BlockSpec((tq, V_DIM), lambda b, h, i: (b * nq + i, h)),
        out_shape=jax.ShapeDtypeStruct((n, heads * V_DIM), BF16),
        compiler_params=_params(("parallel", "parallel", "arbitrary")),
        name="mla_attention",
    )(q, k, v)


def _attn_out_kernel(o_ref, x_ref, w_ref, g_ref, xo_ref):
    m = jnp.dot(o_ref[...], w_ref[...], preferred_element_type=F32)
    xo_ref[...] = x_ref[...] + _rms(m) * g_ref[...]


def _attn_out(o, x, w, g, tm):
    n, d = x.shape
    row = pl.BlockSpec((tm, d), lambda i: (i, 0))
    return pl.pallas_call(
        _attn_out_kernel,
        grid=(n // tm,),
        in_specs=[pl.BlockSpec((tm, o.shape[1]), lambda i: (i, 0)), row,
                  pl.BlockSpec(w.shape, lambda i: (0, 0)), pl.BlockSpec((1, d), lambda i: (0, 0))],
        out_specs=row,
        out_shape=jax.ShapeDtypeStruct((n, d), F32),
        compiler_params=_params(("parallel",)),
        name="mla_out_proj",
    )(o, x, w, g)


def _router_kernel(x_ref, g_ref, whi_ref, wlo_ref, h_ref, ri_ref, rw_ref, cnt_ref, base_ref,
                   *, experts, slabs):
    tm = x_ref.shape[0]

    @pl.when(pl.program_id(0) == 0)
    def _():
        base_ref[...] = jnp.zeros_like(base_ref)

    h = _rms(x_ref[...]) * g_ref[...]
    for f in range(slabs):
        h_ref[pl.ds(f, tm, stride=slabs), :] = h[:, f * LANES:(f + 1) * LANES]

    h_hi = h.astype(BF16)
    h_lo = (h - h_hi.astype(F32)).astype(BF16)
    logits = (jnp.dot(h_hi, whi_ref[...], preferred_element_type=F32)
              + jnp.dot(h_lo, whi_ref[...], preferred_element_type=F32)
              + jnp.dot(h_hi, wlo_ref[...], preferred_element_type=F32))

    lane = lax.broadcasted_iota(jnp.int32, (tm, LANES), 1)
    lg = jnp.where(lane < experts, logits, -jnp.inf)
    m1 = jnp.max(lg, axis=1, keepdims=True)
    i1 = jnp.min(jnp.where(lg == m1, lane, LANES), axis=1, keepdims=True)
    oh1 = lane == i1
    lg2 = jnp.where(oh1, -jnp.inf, lg)
    m2 = jnp.max(lg2, axis=1, keepdims=True)
    i2 = jnp.min(jnp.where(lg2 == m2, lane, LANES), axis=1, keepdims=True)
    oh2 = lane == i2
    e2 = jnp.exp(m2 - m1)
    w1 = 1.0 / (1.0 + e2)
    w2 = e2 / (1.0 + e2)

    cnt = oh1.astype(F32) + oh2.astype(F32)
    r_i = lax.broadcasted_iota(jnp.int32, (tm, tm), 0)
    c_i = lax.broadcasted_iota(jnp.int32, (tm, tm), 1)
    tri = (c_i < r_i).astype(BF16)
    before = jnp.dot(tri, cnt.astype(BF16), preferred_element_type=F32) + base_ref[0:1, :]
    rank1 = jnp.sum(jnp.where(oh1, before, 0.0), axis=1, keepdims=True).astype(jnp.int32)
    rank2 = jnp.sum(jnp.where(oh2, before, 0.0), axis=1, keepdims=True).astype(jnp.int32)
    total = base_ref[0:1, :] + jnp.sum(cnt, axis=0, keepdims=True)
    base_ref[...] = jnp.broadcast_to(total, base_ref.shape)
    cnt_ref[...] = jnp.broadcast_to(total, cnt_ref.shape).astype(jnp.int32)

    ri_ref[...] = jnp.where(lane == 0, i1, jnp.where(lane == 1, i2,
                            jnp.where(lane == 2, rank1, jnp.where(lane == 3, rank2, 0))))
    rw_ref[...] = jnp.where(lane == 0, w1, jnp.where(lane == 1, w2, 0.0))


def _router(x, g, w_router, tm):
    n, d = x.shape
    experts = w_router.shape[1]
    slabs = d // LANES
    w_pad = jnp.zeros((d, LANES), F32).at[:, :experts].set(w_router)
    w_hi = w_pad.astype(BF16)
    w_lo = (w_pad - w_hi.astype(F32)).astype(BF16)
    kern = functools.partial(_router_kernel, experts=experts, slabs=slabs)
    wspec = pl.BlockSpec((d, LANES), lambda i: (0, 0))
    return pl.pallas_call(
        kern,
        grid=(n // tm,),
        in_specs=[pl.BlockSpec((tm, d), lambda i: (i, 0)), pl.BlockSpec((1, d), lambda i: (0, 0)),
                  wspec, wspec],
        out_specs=[pl.BlockSpec((tm * slabs, LANES), lambda i: (i, 0)),
                   pl.BlockSpec((tm, LANES), lambda i: (i, 0)),
                   pl.BlockSpec((tm, LANES), lambda i: (i, 0)),
                   pl.BlockSpec((8, LANES), lambda i: (0, 0))],
        out_shape=[jax.ShapeDtypeStruct((n * slabs, LANES), F32),
                   jax.ShapeDtypeStruct((n, LANES), jnp.int32),
                   jax.ShapeDtypeStruct((n, LANES), F32),
                   jax.ShapeDtypeStruct((8, LANES), jnp.int32)],
        scratch_shapes=[pltpu.VMEM((8, LANES), F32)],
        compiler_params=_params(("arbitrary",)),
        name="moe_router",
    )(x, g, w_hi, w_lo)


def _row_copy(src_ref, dst_ref, sem, src_row, dst_row, slabs):
    return pltpu.make_async_copy(
        src_ref.at[pl.ds(pl.multiple_of(src_row * slabs, slabs), slabs), :],
        dst_ref.at[pl.ds(pl.multiple_of(dst_row * slabs, slabs), slabs), :],
        sem)


def _gather_rows(src_ref, dst_ref, sem, idx_ref, base, count, slabs, wait):
    def body(r, carry):
        cp = _row_copy(src_ref, dst_ref, sem, idx_ref[base + r], r, slabs)
        if wait:
            cp.wait()
        else:
            cp.start()
        return carry

    lax.fori_loop(0, count, body, 0, unroll=8)


def _dispatch_kernel(tok_ref, live_ref, h_ref, o_ref, gbuf_ref, sem, *, slabs):
    i = pl.program_id(0)
    last = pl.num_programs(0) - 1
    tr = o_ref.shape[0]
    slot = i % 2
    nxt = jnp.minimum(i + 1, last)

    @pl.when(jnp.logical_and(i == 0, live_ref[0] == 1))
    def _():
        _gather_rows(h_ref, gbuf_ref.at[0], sem.at[0], tok_ref, 0, tr, slabs, wait=False)

    @pl.when(jnp.logical_and(i < last, live_ref[nxt] == 1))
    def _():
        _gather_rows(h_ref, gbuf_ref.at[1 - slot], sem.at[1 - slot], tok_ref, nxt * tr, tr, slabs,
                     wait=False)

    @pl.when(live_ref[i] == 1)
    def _():
        _gather_rows(h_ref, gbuf_ref.at[slot], sem.at[slot], tok_ref, i * tr, tr, slabs, wait=True)
        for f in range(slabs):
            o_ref[:, f * LANES:(f + 1) * LANES] = (
                gbuf_ref[slot, pl.ds(f, tr, stride=slabs), :].astype(BF16))

    @pl.when(live_ref[i] == 0)
    def _():
        o_ref[...] = jnp.zeros_like(o_ref)


def _dispatch(tok_of_slot, sub_live, h_slabs, d, tr):
    rows = tok_of_slot.shape[0]
    slabs = d // LANES
    grid_spec = pltpu.PrefetchScalarGridSpec(
        num_scalar_prefetch=2,
        grid=(rows // tr,),
        in_specs=[pl.BlockSpec(memory_space=pl.ANY)],
        out_specs=pl.BlockSpec((tr, d), lambda i, tok, live: (i, 0)),
        scratch_shapes=[pltpu.VMEM((2, tr * slabs, LANES), F32), pltpu.SemaphoreType.DMA((2,))],
    )
    return pl.pallas_call(
        functools.partial(_dispatch_kernel, slabs=slabs),
        grid_spec=grid_spec,
        out_shape=jax.ShapeDtypeStruct((rows, d), BF16),
        compiler_params=_params(("arbitrary",)),
        name="moe_dispatch",
    )(tok_of_slot, sub_live, h_slabs)


def _expert_ffn_kernel(pe_ref, ps_ref, pn_ref, used_ref, xs_ref, wg_ref, wu_ref, wd_ref, y_ref,
                       xv_ref, acc_ref, wgu_ref, wdn_ref, stage_ref, sem, *, tr, slabs):
    p = pl.program_id(0)
    c = pl.program_id(1)
    last = pl.num_programs(1) - 1
    tc = wd_ref.shape[0]
    nsub = pn_ref[p]
    sub0 = ps_ref[p]
    live = nsub > 0

    def rows_of(r):
        return pl.ds(pl.multiple_of(r * tr, tr), tr)

    def x_copy(r):
        return pltpu.make_async_copy(
            xs_ref.at[pl.ds(pl.multiple_of((sub0 + r) * tr, tr), tr), :],
            xv_ref.at[rows_of(r), :], sem.at[0])

    def x_start(r, carry):
        x_copy(r).start()
        return carry

    def x_wait(r, carry):
        x_copy(r).wait()
        return carry

    @pl.when(jnp.logical_and(live, c == 0))
    def _():
        lax.fori_loop(0, nsub, x_start, 0)
        lax.fori_loop(0, nsub, x_wait, 0)

    def partial_out(r):
        gu = jnp.dot(xv_ref[rows_of(r), :], wgu_ref[...], preferred_element_type=F32)
        g = gu[:, :tc]
        hid = (g * jax.nn.sigmoid(g) * gu[:, tc:]).astype(BF16)
        return jnp.dot(hid, wdn_ref[...], preferred_element_type=F32)

    def first_apply(r, part):
        acc_ref[rows_of(r), :] = part

    def middle_apply(r, part):
        acc_ref[rows_of(r), :] += part

    def final_apply(r, part):
        res = acc_ref[rows_of(r), :] + part
        for f in range(slabs):
            stage_ref[pl.ds(f, tr, stride=slabs), :] = res[:, f * LANES:(f + 1) * LANES]
        out = pltpu.make_async_copy(
            stage_ref,
            y_ref.at[pl.ds(pl.multiple_of((sub0 + r) * tr * slabs, tr * slabs), tr * slabs), :],
            sem.at[1])
        out.start()
        out.wait()

    def for_each_subtile(apply):
        def pair(q, carry):
            part_a = partial_out(2 * q)
            part_b = partial_out(2 * q + 1)
            apply(2 * q, part_a)
            apply(2 * q + 1, part_b)
            return carry

        lax.fori_loop(0, nsub // 2, pair, 0)

        @pl.when(nsub % 2 == 1)
        def _():
            apply(nsub - 1, partial_out(nsub - 1))

    @pl.when(live)
    def _():
        wgu_ref[:, :tc] = wg_ref[...].astype(BF16)
        wgu_ref[:, tc:] = wu_ref[...].astype(BF16)
        wdn_ref[...] = wd_ref[...].astype(BF16)

        @pl.when(c == 0)
        def _():
            for_each_subtile(first_apply)

        @pl.when(jnp.logical_and(c > 0, c < last))
        def _():
            for_each_subtile(middle_apply)

        @pl.when(c == last)
        def _():
            for_each_subtile(final_apply)

    @pl.when(jnp.logical_and(p == pl.num_programs(0) - 1, c == last))
    def _():
        stage_ref[...] = jnp.zeros_like(stage_ref)

        def zero(s, carry):
            out = pltpu.make_async_copy(
                stage_ref,
                y_ref.at[pl.ds(pl.multiple_of(s * tr * slabs, tr * slabs), tr * slabs), :],
                sem.at[1])
            out.start()
            out.wait()
            return carry

        lax.fori_loop(used_ref[0], y_ref.shape[0] // (tr * slabs), zero, 0)


def _expert_ffn(xs, pass_expert, pass_sub0, pass_nsub, used, w_gu, w_down, layer, tr, cap_sub, tc):
    rows, d = xs.shape
    fe = w_down.shape[2]
    nc = fe // tc
    assert nc >= 2
    slabs = d // LANES
    last = nc - 1

    def chunk(c, pn, p):
        live = jnp.minimum(pn[p], 1)
        return c * live + last * (1 - live)

    grid_spec = pltpu.PrefetchScalarGridSpec(
        num_scalar_prefetch=4,
        grid=(pass_expert.shape[0], nc),
        in_specs=[
            pl.BlockSpec(memory_space=pl.ANY),
            pl.BlockSpec((None, None, d, tc),
                         lambda p, c, pe, ps, pn, us: (layer, pe[p], 0, chunk(c, pn, p))),
            pl.BlockSpec((None, None, d, tc),
                         lambda p, c, pe, ps, pn, us: (layer, pe[p], 0, nc + chunk(c, pn, p))),
            pl.BlockSpec((None, None, tc, d),
                         lambda p, c, pe, ps, pn, us: (layer, pe[p], chunk(c, pn, p), 0)),
        ],
        out_specs=pl.BlockSpec(memory_space=pl.ANY),
        scratch_shapes=[pltpu.VMEM((cap_sub * tr, d), BF16), pltpu.VMEM((cap_sub * tr, d), F32),
                        pltpu.VMEM((d, 2 * tc), BF16), pltpu.VMEM((tc, d), BF16),
                        pltpu.VMEM((tr * slabs, LANES), F32), pltpu.SemaphoreType.DMA((2,))],
    )
    return pl.pallas_call(
        functools.partial(_expert_ffn_kernel, tr=tr, slabs=slabs),
        grid_spec=grid_spec,
        out_shape=jax.ShapeDtypeStruct((rows * slabs, LANES), F32),
        compiler_params=_params(("arbitrary", "arbitrary")),
        name="moe_expert_swiglu",
    )(pass_expert, pass_sub0, pass_nsub, used, xs, w_gu, w_gu, w_down)


def _combine_kernel(s1_ref, s2_ref, y_ref, rw_ref, x_ref, g_ref, o_ref, b1_ref, b2_ref, sem,
                    *, slabs):
    i = pl.program_id(0)
    last = pl.num_programs(0) - 1
    tm = x_ref.shape[0]
    slot = i % 2

    def fetch(tile, s, wait):
        _gather_rows(y_ref, b1_ref.at[s], sem.at[0, s], s1_ref, tile * tm, tm, slabs, wait)
        _gather_rows(y_ref, b2_ref.at[s], sem.at[1, s], s2_ref, tile * tm, tm, slabs, wait)

    @pl.when(i == 0)
    def _():
        fetch(0, 0, False)

    @pl.when(i < last)
    def _():
        fetch(i + 1, 1 - slot, False)

    fetch(i, slot, True)

    rw = rw_ref[...]
    w1 = rw[:, 0:1]
    w2 = rw[:, 1:2]
    parts = []
    ssq = jnp.zeros((tm, 1), F32)
    for f in range(slabs):
        m = (w1 * b1_ref[slot, pl.ds(f, tm, stride=slabs), :]
             + w2 * b2_ref[slot, pl.ds(f, tm, stride=slabs), :])
        ssq = ssq + jnp.sum(m * m, axis=1, keepdims=True)
        parts.append(m)
    inv = lax.rsqrt(ssq / (slabs * LANES) + NORM_EPS)
    for f in range(slabs):
        sl = slice(f * LANES, (f + 1) * LANES)
        o_ref[:, sl] = x_ref[:, sl] + parts[f] * inv * g_ref[:, sl]


def _combine(slot1, slot2, y_slabs, rw, x, g, tm):
    n, d = x.shape
    slabs = d // LANES
    grid_spec = pltpu.PrefetchScalarGridSpec(
        num_scalar_prefetch=2,
        grid=(n // tm,),
        in_specs=[
            pl.BlockSpec(memory_space=pl.ANY),
            pl.BlockSpec((tm, LANES), lambda i, s1, s2: (i, 0)),
            pl.BlockSpec((tm, d), lambda i, s1, s2: (i, 0)),
            pl.BlockSpec((1, d), lambda i, s1, s2: (0, 0)),
        ],
        out_specs=pl.BlockSpec((tm, d), lambda i, s1, s2: (i, 0)),
        scratch_shapes=[pltpu.VMEM((2, tm * slabs, LANES), F32),
                        pltpu.VMEM((2, tm * slabs, LANES), F32),
                        pltpu.SemaphoreType.DMA((2, 2))],
    )
    return pl.pallas_call(
        functools.partial(_combine_kernel, slabs=slabs),
        grid_spec=grid_spec,
        out_shape=jax.ShapeDtypeStruct((n, d), F32),
        compiler_params=_params(("arbitrary",)),
        name="moe_combine",
    )(slot1, slot2, y_slabs, rw, x, g)


def _moe_ffn(x, g_in, g_out, w_router, w_gu, w_down, layer, tiles):
    n, d = x.shape
    experts = w_router.shape[1]
    tr = tiles["expert_rows"]
    cap_sub = tiles["expert_cap"] // tr
    h_slabs, ri, rw, cnt = _router(x, g_in, w_router, tiles["router_rows"])

    counts = cnt[0, :experts]
    nsub = (counts + tr - 1) // tr
    sub_end = jnp.cumsum(nsub)
    sub_start = sub_end - nsub
    slot1 = (sub_start[ri[:, 0]] * tr + ri[:, 2]).astype(jnp.int32)
    slot2 = (sub_start[ri[:, 1]] * tr + ri[:, 3]).astype(jnp.int32)
    rows = TOP_K * n + experts * tr
    n_sub = rows // tr
    token = jnp.arange(n, dtype=jnp.int32)
    tok_of_slot = jnp.zeros((rows,), jnp.int32).at[slot1].set(token).at[slot2].set(token)
    sub_live = (jnp.arange(n_sub, dtype=jnp.int32) < sub_end[-1]).astype(jnp.int32)

    n_pass = experts + n_sub // cap_sub
    passes = (nsub + cap_sub - 1) // cap_sub
    pass_end = jnp.cumsum(passes)
    pidx = jnp.arange(n_pass, dtype=jnp.int32)
    pe = jnp.minimum(jnp.searchsorted(pass_end, pidx, side="right"), experts - 1)
    local = pidx - (pass_end[pe] - passes[pe])
    p_live = pidx < pass_end[-1]
    pass_nsub = jnp.where(p_live, jnp.minimum(cap_sub, nsub[pe] - local * cap_sub), 0)
    pass_sub0 = jnp.where(p_live, sub_start[pe] + local * cap_sub, 0)
    last_pe = jnp.max(jnp.where(passes > 0, jnp.arange(experts), 0))
    pass_expert = jnp.where(p_live, pe, last_pe)

    xs = _dispatch(tok_of_slot, sub_live, h_slabs, d, tr)
    y_slabs = _expert_ffn(xs, pass_expert.astype(jnp.int32), pass_sub0.astype(jnp.int32),
                          pass_nsub.astype(jnp.int32), sub_end[-1:].astype(jnp.int32),
                          w_gu, w_down, layer, tr, cap_sub,
                          tiles["expert_chunk"])
    return _combine(slot1, slot2, y_slabs, rw, x, g_out, tiles["combine_rows"])


def _tiles(n, seq, d, ffn, expert_ffn):
    pick = lambda pref, dim: min(pref, dim)
    return {
        "pw1_rows": pick(512, n), "pw1_cols": pick(512, d),
        "conv_rows": pick(1024, seq), "conv_cols": pick(256, d),
        "ln_rows": pick(256, n),
        "ffn_rows": pick(512, n), "ffn_chunk": pick(512, ffn),
        "proj_rows": pick(256, seq),
        "attn_q": pick(512, seq), "attn_k": pick(512, seq),
        "out_rows": pick(256, n),
        "router_rows": pick(512, n),
        "expert_rows": pick(256, n), "expert_cap": pick(2304, TOP_K * n),
        "expert_chunk": pick(256, expert_ffn),
        "combine_rows": pick(256, n),
    }


def _rope_tables(seq):
    pos = jnp.arange(seq, dtype=F32)
    inv_freq = ROPE_THETA ** (-jnp.arange(0, QK_ROPE_DIM, 2, dtype=F32) / QK_ROPE_DIM)
    ang = pos[:, None] * inv_freq[None, :]
    cos, sin = jnp.cos(ang), jnp.sin(ang)
    zeros = jnp.zeros((seq, LANES - QK_ROPE_DIM), F32)
    return (jnp.concatenate([cos, cos, zeros], axis=1),
            jnp.concatenate([-sin, sin, zeros], axis=1))


def _swap_halves(w):
    half = w.shape[-1] // 2
    return jnp.concatenate([w[..., half:], w[..., :half]], axis=-1)


def kernel(x, norm_g, conv_w_pw1, conv_b_pw1, conv_w_dw, conv_b_dw, conv_ln_g, conv_ln_b,
           conv_w_pw2, ffn_w_gu, ffn_w_down, moe_w_router, moe_w_gu, moe_w_down,
           kv_in_g, kv_w_a, kv_latent_g, kv_w_b, attn_w_q_a, attn_q_latent_g,
           attn_w_q_b, attn_w_o):
    batch, seq, d = x.shape
    n = batch * seq
    depth = norm_g.shape[0]
    n_conv = depth // 2
    heads = d // 128
    kv_rank = kv_latent_g.shape[0]
    tiles = _tiles(n, seq, d, ffn_w_down.shape[1], moe_w_down.shape[2])
    cos2, sin2 = _rope_tables(seq)
    vec = lambda v: v.reshape(1, -1)

    xs = x.reshape(n, d)
    k = v = None
    for i in range(depth):
        if i == n_conv:
            rope_w = kv_w_a[:, kv_rank:]
            wa_ext = jnp.concatenate([kv_w_a, _swap_halves(rope_w)], axis=1).astype(BF16)
            wb = kv_w_b.reshape(kv_rank, heads, QK_NOPE_DIM + V_DIM)
            wb_split = jnp.concatenate(
                [wb[:, :, :QK_NOPE_DIM].reshape(kv_rank, heads * QK_NOPE_DIM),
                 wb[:, :, QK_NOPE_DIM:].reshape(kv_rank, heads * V_DIM)], axis=1).astype(BF16)
            k, v = _kv_proj(xs, vec(kv_in_g), wa_ext, vec(kv_latent_g), wb_split, cos2, sin2,
                            seq, tiles["proj_rows"])
        if i < n_conv:
            u = _pw1_glu(xs, vec(norm_g[i, 0]), conv_w_pw1[i].astype(BF16), vec(conv_b_pw1[i]),
                         tiles["pw1_rows"], tiles["pw1_cols"])
            c = _dwconv(u, conv_w_dw[i], vec(conv_b_dw[i]), seq, tiles["conv_rows"],
                        tiles["conv_cols"])
            xs = _ln_pw2(c, xs, vec(conv_ln_g[i]), vec(conv_ln_b[i]), conv_w_pw2[i].astype(BF16),
                         vec(norm_g[i, 1]), tiles["ln_rows"])
        else:
            j = i - n_conv
            q_rank = attn_w_q_a.shape[2]
            wqb = attn_w_q_b[j].reshape(q_rank, heads, QK_NOPE_DIM + QK_ROPE_DIM)
            rope_w = wqb[:, :, QK_NOPE_DIM:]
            wqb_ext = jnp.concatenate([wqb, _swap_halves(rope_w)], axis=2)
            wqb_ext = wqb_ext.reshape(q_rank, heads * 256).astype(BF16)
            q = _q_proj(xs, vec(norm_g[i, 0]), attn_w_q_a[j].astype(BF16),
                        vec(attn_q_latent_g[j]), wqb_ext, cos2, sin2, seq, tiles["proj_rows"])
            o = _attention(q, k, v, batch, seq, tiles["attn_q"], tiles["attn_k"])
            xs = _attn_out(o, xs, attn_w_o[j].astype(BF16), vec(norm_g[i, 1]), tiles["out_rows"])
        if i % 2 == 0:
            xs = _dense_ffn(xs, vec(norm_g[i, 2]), vec(norm_g[i, 3]), ffn_w_gu[i // 2].astype(BF16),
                            ffn_w_down[i // 2].astype(BF16), tiles["ffn_rows"], tiles["ffn_chunk"])
        else:
            xs = _moe_ffn(xs, vec(norm_g[i, 2]), vec(norm_g[i, 3]), moe_w_router[i // 2],
                          moe_w_gu, moe_w_down, i // 2, tiles)
    return xs.reshape(batch, seq, d)
```

```python
import functools

import jax
import jax.numpy as jnp
from jax import lax
from jax.experimental import pallas as pl
from jax.experimental.pallas import tpu as pltpu

NORM_EPS = 1e-6
ROPE_THETA = 10000.0
QK_NOPE_DIM = 128
QK_ROPE_DIM = 64
V_DIM = 128
TOP_K = 2

LANES = 128
SUBLANES = 8
CONV_HALO = 32
GATHER_UNROLL = 8
VMEM_LIMIT = 56 * 1024 * 1024

F32 = jnp.float32
BF16 = jnp.bfloat16


def _params(semantics):
    return pltpu.CompilerParams(dimension_semantics=semantics, vmem_limit_bytes=VMEM_LIMIT)


def _rms(x):
    return x * lax.rsqrt(jnp.mean(x * x, axis=-1, keepdims=True) + NORM_EPS)


def _pw1_glu_kernel(x_ref, g_ref, wa_ref, wg_ref, ba_ref, bg_ref, o_ref, xn_ref):
    @pl.when(pl.program_id(1) == 0)
    def _():
        xn_ref[...] = (_rms(x_ref[...]) * g_ref[...]).astype(BF16)

    xn = xn_ref[...]
    a = jnp.dot(xn, wa_ref[...], preferred_element_type=F32) + ba_ref[...]
    g = jnp.dot(xn, wg_ref[...], preferred_element_type=F32) + bg_ref[...]
    o_ref[...] = (a * jax.nn.sigmoid(g)).astype(o_ref.dtype)


def _pw1_glu(x, g, w, b, tm, tn):
    n, d = x.shape
    nj = d // tn
    return pl.pallas_call(
        _pw1_glu_kernel,
        grid=(n // tm, nj),
        in_specs=[
            pl.BlockSpec((tm, d), lambda i, j: (i, 0)),
            pl.BlockSpec((1, d), lambda i, j: (0, 0)),
            pl.BlockSpec((d, tn), lambda i, j: (0, j)),
            pl.BlockSpec((d, tn), lambda i, j: (0, nj + j)),
            pl.BlockSpec((1, tn), lambda i, j: (0, j)),
            pl.BlockSpec((1, tn), lambda i, j: (0, nj + j)),
        ],
        out_specs=pl.BlockSpec((tm, tn), lambda i, j: (i, j)),
        out_shape=jax.ShapeDtypeStruct((n, d), BF16),
        scratch_shapes=[pltpu.VMEM((tm, d), BF16)],
        compiler_params=_params(("parallel", "arbitrary")),
        name="conv_pw1_glu",
    )(x, g, w, w, b, b)


def _dwconv_kernel(u_ref, halo_ref, w_ref, b_ref, o_ref, sh_ref, *, tiles_per_seq, width, rb):
    tm = u_ref.shape[0]
    ext = tm + CONV_HALO
    first = (pl.program_id(0) % tiles_per_seq) == 0
    halo = halo_ref[...].astype(F32)
    sh_ref[0, 0:CONV_HALO, :] = jnp.where(first, jnp.zeros_like(halo), halo)
    sh_ref[0, CONV_HALO:ext, :] = u_ref[...].astype(F32)
    for j in range(1, SUBLANES):
        sh_ref[j, 0:ext - SUBLANES, :] = sh_ref[0, j:ext - SUBLANES + j, :]
    base = CONV_HALO - (width - 1)

    def block(blk, carry):
        r0 = pl.multiple_of(blk * rb, rb)
        acc = None
        for k in range(width):
            j = (base + k) % SUBLANES
            a = (base + k) - j
            term = w_ref[k:k + 1, :] * sh_ref[j, pl.ds(r0 + a, rb), :]
            acc = term if acc is None else acc + term
        o_ref[pl.ds(r0, rb), :] = (acc + b_ref[...]).astype(o_ref.dtype)
        return carry

    lax.fori_loop(0, tm // rb, block, 0)


def _dwconv(u, w, b, seq, tm, tc):
    n, d = u.shape
    width = w.shape[0]
    hb = tm // CONV_HALO
    kern = functools.partial(_dwconv_kernel, tiles_per_seq=seq // tm, width=width, rb=min(64, tm))
    return pl.pallas_call(
        kern,
        grid=(n // tm, d // tc),
        in_specs=[
            pl.BlockSpec((tm, tc), lambda i, j: (i, j)),
            pl.BlockSpec((CONV_HALO, tc), lambda i, j: (jnp.maximum(i * hb - 1, 0), j)),
            pl.BlockSpec((width, tc), lambda i, j: (0, j)),
            pl.BlockSpec((1, tc), lambda i, j: (0, j)),
        ],
        out_specs=pl.BlockSpec((tm, tc), lambda i, j: (i, j)),
        out_shape=jax.ShapeDtypeStruct((n, d), BF16),
        scratch_shapes=[pltpu.VMEM((SUBLANES, tm + CONV_HALO, tc), F32)],
        compiler_params=_params(("parallel", "parallel")),
        name="conv_depthwise",
    )(u, u, w, b)


def _ln_pw2_kernel(c_ref, x_ref, lg_ref, lb_ref, w_ref, g_ref, o_ref):
    c = c_ref[...].astype(F32)
    mu = jnp.mean(c, axis=-1, keepdims=True)
    cc = c - mu
    y = cc * lax.rsqrt(jnp.mean(cc * cc, axis=-1, keepdims=True) + NORM_EPS)
    y = y * lg_ref[...] + lb_ref[...]
    y = (y * jax.nn.sigmoid(y)).astype(BF16)
    m = jnp.dot(y, w_ref[...], preferred_element_type=F32)
    o_ref[...] = x_ref[...] + _rms(m) * g_ref[...]


def _ln_pw2(c, x, ln_g, ln_b, w, g, tm):
    n, d = x.shape
    row = pl.BlockSpec((tm, d), lambda i: (i, 0))
    vec = pl.BlockSpec((1, d), lambda i: (0, 0))
    return pl.pallas_call(
        _ln_pw2_kernel,
        grid=(n // tm,),
        in_specs=[row, row, vec, vec, pl.BlockSpec((d, d), lambda i: (0, 0)), vec],
        out_specs=row,
        out_shape=jax.ShapeDtypeStruct((n, d), F32),
        compiler_params=_params(("parallel",)),
        name="conv_ln_pw2",
    )(c, x, ln_g, ln_b, w, g)


def _swiglu_chunk(xn_ref, wg_ref, wu_ref, wd_ref, acc_ref, groups):
    rows = xn_ref.shape[0] // groups
    hids = []
    for i in range(groups):
        xn = xn_ref[i * rows:(i + 1) * rows, :]
        g = jnp.dot(xn, wg_ref[...], preferred_element_type=F32)
        u = jnp.dot(xn, wu_ref[...], preferred_element_type=F32)
        hids.append((g * jax.nn.sigmoid(g) * u).astype(BF16))
    for i in range(groups):
        acc_ref[i * rows:(i + 1) * rows, :] += jnp.dot(hids[i], wd_ref[...],
                                                       preferred_element_type=F32)


def _dense_ffn_kernel(x_ref, gi_ref, go_ref, wg_ref, wu_ref, wd_ref, o_ref, xn_ref, acc_ref):
    c = pl.program_id(1)

    @pl.when(c == 0)
    def _():
        xn_ref[...] = (_rms(x_ref[...]) * gi_ref[...]).astype(BF16)
        acc_ref[...] = jnp.zeros_like(acc_ref)

    _swiglu_chunk(xn_ref, wg_ref, wu_ref, wd_ref, acc_ref, groups=2)

    @pl.when(c == pl.num_programs(1) - 1)
    def _():
        o_ref[...] = x_ref[...] + _rms(acc_ref[...]) * go_ref[...]


def _dense_ffn(x, g_in, g_out, w_gu, w_down, tm, tc):
    n, d = x.shape
    f = w_down.shape[0]
    nc = f // tc
    row = pl.BlockSpec((tm, d), lambda i, c: (i, 0))
    vec = pl.BlockSpec((1, d), lambda i, c: (0, 0))
    return pl.pallas_call(
        _dense_ffn_kernel,
        grid=(n // tm, nc),
        in_specs=[
            row, vec, vec,
            pl.BlockSpec((d, tc), lambda i, c: (0, c)),
            pl.BlockSpec((d, tc), lambda i, c: (0, nc + c)),
            pl.BlockSpec((tc, d), lambda i, c: (c, 0)),
        ],
        out_specs=row,
        out_shape=jax.ShapeDtypeStruct((n, d), F32),
        scratch_shapes=[pltpu.VMEM((tm, d), BF16), pltpu.VMEM((tm, d), F32)],
        compiler_params=_params(("parallel", "arbitrary")),
        name="dense_swiglu",
    )(x, g_in, g_out, w_gu, w_gu, w_down)


def _rope128(t, cos2, sin2):
    return t * cos2 + pltpu.roll(t, 64, axis=1) * sin2


def _kv_proj_kernel(x_ref, g_ref, wa_ref, lg_ref, wb_ref, cos_ref, sin_ref, k_ref, v_ref,
                    *, heads, rank):
    h = (_rms(x_ref[...]) * g_ref[...]).astype(BF16)
    a = jnp.dot(h, wa_ref[...], preferred_element_type=F32)
    c_kv = (_rms(a[:, :rank]) * lg_ref[...]).astype(BF16)
    k_rope = _rope128(a[:, rank:], cos_ref[...], sin_ref[...]).astype(BF16)
    kv = jnp.dot(c_kv, wb_ref[...], preferred_element_type=F32)
    hd = heads * QK_NOPE_DIM
    for hh in range(heads):
        k_ref[:, hh * 256:hh * 256 + 128] = kv[:, hh * 128:(hh + 1) * 128].astype(BF16)
        k_ref[:, hh * 256 + 128:(hh + 1) * 256] = k_rope
    v_ref[...] = kv[:, hd:].astype(BF16)


def _kv_proj(x, g, wa_ext, lat_g, wb_split, cos2, sin2, seq, tm):
    n, d = x.shape
    heads = d // 128
    rank = lat_g.shape[1]
    spt = seq // tm
    kern = functools.partial(_kv_proj_kernel, heads=heads, rank=rank)
    full = lambda a: pl.BlockSpec(a.shape, lambda i: (0, 0))
    return pl.pallas_call(
        kern,
        grid=(n // tm,),
        in_specs=[
            pl.BlockSpec((tm, d), lambda i: (i, 0)), full(g), full(wa_ext), full(lat_g),
            full(wb_split),
            pl.BlockSpec((tm, LANES), lambda i: (i % spt, 0)),
            pl.BlockSpec((tm, LANES), lambda i: (i % spt, 0)),
        ],
        out_specs=[pl.BlockSpec((tm, heads * 256), lambda i: (i, 0)),
                   pl.BlockSpec((tm, heads * 128), lambda i: (i, 0))],
        out_shape=[jax.ShapeDtypeStruct((n, heads * 256), BF16),
                   jax.ShapeDtypeStruct((n, heads * 128), BF16)],
        compiler_params=_params(("parallel",)),
        name="mla_kv_proj",
    )(x, g, wa_ext, lat_g, wb_split, cos2, sin2)


def _q_proj_kernel(x_ref, g_ref, wa_ref, lg_ref, wb_ref, cos_ref, sin_ref, q_ref, *, heads, scale):
    h = (_rms(x_ref[...]) * g_ref[...]).astype(BF16)
    lat = jnp.dot(h, wa_ref[...], preferred_element_type=F32)
    lat = (_rms(lat) * lg_ref[...]).astype(BF16)
    q = jnp.dot(lat, wb_ref[...], preferred_element_type=F32)
    cos2 = cos_ref[...]
    sin2 = sin_ref[...]
    for hh in range(heads):
        q_ref[:, hh * 256:hh * 256 + 128] = (q[:, hh * 256:hh * 256 + 128] * scale).astype(BF16)
        r = _rope128(q[:, hh * 256 + 128:(hh + 1) * 256], cos2, sin2)
        q_ref[:, hh * 256 + 128:(hh + 1) * 256] = (r * scale).astype(BF16)


def _q_proj(x, g, wa, lat_g, wb_ext, cos2, sin2, seq, tm):
    n, d = x.shape
    heads = d // 128
    spt = seq // tm
    scale = float(QK_NOPE_DIM + QK_ROPE_DIM) ** -0.5 * 1.4426950408889634
    kern = functools.partial(_q_proj_kernel, heads=heads, scale=scale)
    full = lambda a: pl.BlockSpec(a.shape, lambda i: (0, 0))
    return pl.pallas_call(
        kern,
        grid=(n // tm,),
        in_specs=[
            pl.BlockSpec((tm, d), lambda i: (i, 0)), full(g), full(wa), full(lat_g), full(wb_ext),
            pl.BlockSpec((tm, LANES), lambda i: (i % spt, 0)),
            pl.BlockSpec((tm, LANES), lambda i: (i % spt, 0)),
        ],
        out_specs=pl.BlockSpec((tm, heads * 256), lambda i: (i, 0)),
        out_shape=jax.ShapeDtypeStruct((n, heads * 256), BF16),
        compiler_params=_params(("parallel",)),
        name="mla_q_proj",
    )(x, g, wa, lat_g, wb_ext, cos2, sin2)


def _attn_kernel(q_ref, k_ref, v_ref, o_ref, *, tk, parts):
    tq = q_ref.shape[0]
    rows = tq // parts
    qi = pl.program_id(2)
    qs = [q_ref[g * rows:(g + 1) * rows, :] for g in range(parts)]
    row = lax.broadcasted_iota(jnp.int32, (rows, tk), 0)
    col = lax.broadcasted_iota(jnp.int32, (rows, tk), 1)

    def tile(j, carry, masked):
        start = pl.multiple_of(j * tk, tk)
        k = k_ref[pl.ds(start, tk), :]
        v = v_ref[pl.ds(start, tk), :]
        out = []
        scores = [lax.dot_general(qs[g], k, (((1,), (1,)), ((), ())), preferred_element_type=F32)
                  for g in range(parts)]
        for g in range(parts):
            m, l, acc = carry[g]
            s = scores[g]
            if masked:
                s = jnp.where(row + (qi * tq + g * rows) >= col + start, s, -jnp.inf)
            m_new = jnp.maximum(m, jnp.max(s, axis=1, keepdims=True))
            p = jnp.exp2(s - m_new)
            alpha = jnp.exp2(m - m_new)
            l = alpha * l + jnp.sum(p, axis=1, keepdims=True)
            pv = jnp.dot(p.astype(BF16), v, preferred_element_type=F32)
            out.append((m_new, l, alpha * acc + pv))
        return tuple(out)

    init = tuple((jnp.full((rows, 1), -jnp.inf, F32), jnp.zeros((rows, 1), F32),
                  jnp.zeros((rows, V_DIM), F32)) for _ in range(parts))
    n_full = qi * (tq // tk)
    carry = lax.fori_loop(0, n_full, lambda j, c: tile(j, c, False), init)
    for d in range(tq // tk):
        carry = tile(n_full + d, carry, True)
    for g in range(parts):
        _, l, acc = carry[g]
        o_ref[g * rows:(g + 1) * rows, :] = (acc / l).astype(o_ref.dtype)


def _attention(q, k, v, batch, seq, tq, tk):
    n = q.shape[0]
    heads = v.shape[1] // V_DIM
    nq = seq // tq
    kern = functools.partial(_attn_kernel, tk=tk, parts=1)
    return pl.pallas_call(
        kern,
        grid=(batch, heads, nq),
        in_specs=[
            pl.BlockSpec((tq, 256), lambda b, h, i: (b * nq + i, h)),
            pl.BlockSpec((seq, 256), lambda b, h, i: (b, h)),
            pl.BlockSpec((seq, V_DIM), lambda b, h, i: (b, h)),
        ],
        out_specs=pl.BlockSpec((tq, V_DIM), lambda b, h, i: (b * nq + i, h)),
        out_shape=jax.ShapeDtypeStruct((n, heads * V_DIM), BF16),
        compiler_params=_params(("parallel", "parallel", "arbitrary")),
        name="mla_attention",
    )(q, k, v)


def _attn_out_kernel(o_ref, x_ref, w_ref, g_ref, xo_ref):
    m = jnp.dot(o_ref[...], w_ref[...], preferred_element_type=F32)
    xo_ref[...] = x_ref[...] + _rms(m) * g_ref[...]


def _attn_out(o, x, w, g, tm):
    n, d = x.shape
    row = pl.BlockSpec((tm, d), lambda i: (i, 0))
    return pl.pallas_call(
        _attn_out_kernel,
        grid=(n // tm,),
        in_specs=[pl.BlockSpec((tm, o.shape[1]), lambda i: (i, 0)), row,
                  pl.BlockSpec(w.shape, lambda i: (0, 0)), pl.BlockSpec((1, d), lambda i: (0, 0))],
        out_specs=row,
        out_shape=jax.ShapeDtypeStruct((n, d), F32),
        compiler_params=_params(("parallel",)),
        name="mla_out_proj",
    )(o, x, w, g)


def _router_kernel(x_ref, g_ref, whi_ref, wlo_ref, h_ref, ri_ref, rw_ref, cnt_ref, base_ref,
                   *, experts, slabs):
    tm = x_ref.shape[0]

    @pl.when(pl.program_id(0) == 0)
    def _():
        base_ref[...] = jnp.zeros_like(base_ref)

    h = _rms(x_ref[...]) * g_ref[...]
    for f in range(slabs):
        h_ref[pl.ds(f, tm, stride=slabs), :] = h[:, f * LANES:(f + 1) * LANES]

    h_hi = h.astype(BF16)
    h_lo = (h - h_hi.astype(F32)).astype(BF16)
    logits = (jnp.dot(h_hi, whi_ref[...], preferred_element_type=F32)
              + jnp.dot(h_lo, whi_ref[...], preferred_element_type=F32)
              + jnp.dot(h_hi, wlo_ref[...], preferred_element_type=F32))

    lane = lax.broadcasted_iota(jnp.int32, (tm, LANES), 1)
    lg = jnp.where(lane < experts, logits, -jnp.inf)
    m1 = jnp.max(lg, axis=1, keepdims=True)
    i1 = jnp.min(jnp.where(lg == m1, lane, LANES), axis=1, keepdims=True)
    oh1 = lane == i1
    lg2 = jnp.where(oh1, -jnp.inf, lg)
    m2 = jnp.max(lg2, axis=1, keepdims=True)
    i2 = jnp.min(jnp.where(lg2 == m2, lane, LANES), axis=1, keepdims=True)
    oh2 = lane == i2
    e2 = jnp.exp(m2 - m1)
    w1 = 1.0 / (1.0 + e2)
    w2 = e2 / (1.0 + e2)

    cnt = oh1.astype(F32) + oh2.astype(F32)
    r_i = lax.broadcasted_iota(jnp.int32, (tm, tm), 0)
    c_i = lax.broadcasted_iota(jnp.int32, (tm, tm), 1)
    tri = (c_i < r_i).astype(BF16)
    before = jnp.dot(tri, cnt.astype(BF16), preferred_element_type=F32) + base_ref[0:1, :]
    rank1 = jnp.sum(jnp.where(oh1, before, 0.0), axis=1, keepdims=True).astype(jnp.int32)
    rank2 = jnp.sum(jnp.where(oh2, before, 0.0), axis=1, keepdims=True).astype(jnp.int32)
    total = base_ref[0:1, :] + jnp.sum(cnt, axis=0, keepdims=True)
    base_ref[...] = jnp.broadcast_to(total, base_ref.shape)
    cnt_ref[...] = jnp.broadcast_to(total, cnt_ref.shape).astype(jnp.int32)

    ri_ref[...] = jnp.where(lane == 0, i1, jnp.where(lane == 1, i2,
                            jnp.where(lane == 2, rank1, jnp.where(lane == 3, rank2, 0))))
    rw_ref[...] = jnp.where(lane == 0, w1, jnp.where(lane == 1, w2, 0.0))


def _router(x, g, w_router, tm):
    n, d = x.shape
    experts = w_router.shape[1]
    slabs = d // LANES
    w_pad = jnp.zeros((d, LANES), F32).at[:, :experts].set(w_router)
    w_hi = w_pad.astype(BF16)
    w_lo = (w_pad - w_hi.astype(F32)).astype(BF16)
    kern = functools.partial(_router_kernel, experts=experts, slabs=slabs)
    wspec = pl.BlockSpec((d, LANES), lambda i: (0, 0))
    return pl.pallas_call(
        kern,
        grid=(n // tm,),
        in_specs=[pl.BlockSpec((tm, d), lambda i: (i, 0)), pl.BlockSpec((1, d), lambda i: (0, 0)),
                  wspec, wspec],
        out_specs=[pl.BlockSpec((tm * slabs, LANES), lambda i: (i, 0)),
                   pl.BlockSpec((tm, LANES), lambda i: (i, 0)),
                   pl.BlockSpec((tm, LANES), lambda i: (i, 0)),
                   pl.BlockSpec((8, LANES), lambda i: (0, 0))],
        out_shape=[jax.ShapeDtypeStruct((n * slabs, LANES), F32),
                   jax.ShapeDtypeStruct((n, LANES), jnp.int32),
                   jax.ShapeDtypeStruct((n, LANES), F32),
                   jax.ShapeDtypeStruct((8, LANES), jnp.int32)],
        scratch_shapes=[pltpu.VMEM((8, LANES), F32)],
        compiler_params=_params(("arbitrary",)),
        name="moe_router",
    )(x, g, w_hi, w_lo)


def _row_copy(src_ref, dst_ref, sem, src_row, dst_row, slabs):
    return pltpu.make_async_copy(
        src_ref.at[pl.ds(pl.multiple_of(src_row * slabs, slabs), slabs), :],
        dst_ref.at[pl.ds(pl.multiple_of(dst_row * slabs, slabs), slabs), :],
        sem)


def _gather_rows(src_ref, dst_ref, sem, idx_ref, base, count, slabs, wait):
    def body(q, carry):
        for j in range(GATHER_UNROLL):
            r = q * GATHER_UNROLL + j
            cp = _row_copy(src_ref, dst_ref, sem, idx_ref[base + r], r, slabs)
            if wait:
                cp.wait()
            else:
                cp.start(priority=j % 2)
        return carry

    assert count % GATHER_UNROLL == 0
    lax.fori_loop(0, count // GATHER_UNROLL, body, 0)


def _dispatch_kernel(tok_ref, live_ref, h_ref, o_ref, gbuf_ref, sem, *, slabs):
    i = pl.program_id(0)
    last = pl.num_programs(0) - 1
    tr = o_ref.shape[0]
    slot = i % 2
    nxt = jnp.minimum(i + 1, last)

    @pl.when(jnp.logical_and(i == 0, live_ref[0] == 1))
    def _():
        _gather_rows(h_ref, gbuf_ref.at[0], sem.at[0], tok_ref, 0, tr, slabs, wait=False)

    @pl.when(jnp.logical_and(i < last, live_ref[nxt] == 1))
    def _():
        _gather_rows(h_ref, gbuf_ref.at[1 - slot], sem.at[1 - slot], tok_ref, nxt * tr, tr, slabs,
                     wait=False)

    @pl.when(live_ref[i] == 1)
    def _():
        _gather_rows(h_ref, gbuf_ref.at[slot], sem.at[slot], tok_ref, i * tr, tr, slabs, wait=True)
        for f in range(slabs):
            o_ref[:, f * LANES:(f + 1) * LANES] = (
                gbuf_ref[slot, pl.ds(f, tr, stride=slabs), :].astype(BF16))

    @pl.when(live_ref[i] == 0)
    def _():
        o_ref[...] = jnp.zeros_like(o_ref)


def _dispatch(tok_of_slot, sub_live, h_slabs, d, tr):
    rows = tok_of_slot.shape[0]
    slabs = d // LANES
    grid_spec = pltpu.PrefetchScalarGridSpec(
        num_scalar_prefetch=2,
        grid=(rows // tr,),
        in_specs=[pl.BlockSpec(memory_space=pl.ANY)],
        out_specs=pl.BlockSpec((tr, d), lambda i, tok, live: (i, 0)),
        scratch_shapes=[pltpu.VMEM((2, tr * slabs, LANES), F32), pltpu.SemaphoreType.DMA((2,))],
    )
    return pl.pallas_call(
        functools.partial(_dispatch_kernel, slabs=slabs),
        grid_spec=grid_spec,
        out_shape=jax.ShapeDtypeStruct((rows, d), BF16),
        compiler_params=_params(("arbitrary",)),
        name="moe_dispatch",
    )(tok_of_slot, sub_live, h_slabs)


def _expert_ffn_kernel(pe_ref, ps_ref, pn_ref, used_ref, xs_ref, wg_ref, wu_ref, wd_ref, y_ref,
                       xv_ref, acc_ref, wgu_ref, wdn_ref, stage_ref, sem, *, tr, slabs):
    p = pl.program_id(0)
    c = pl.program_id(1)
    last = pl.num_programs(1) - 1
    tc = wd_ref.shape[0]
    nsub = pn_ref[p]
    sub0 = ps_ref[p]
    live = nsub > 0

    def rows_of(r):
        return pl.ds(pl.multiple_of(r * tr, tr), tr)

    def x_copy(r):
        return pltpu.make_async_copy(
            xs_ref.at[pl.ds(pl.multiple_of((sub0 + r) * tr, tr), tr), :],
            xv_ref.at[rows_of(r), :], sem.at[0])

    def x_start(r, carry):
        x_copy(r).start()
        return carry

    def x_wait(r, carry):
        x_copy(r).wait()
        return carry

    @pl.when(jnp.logical_and(live, c == 0))
    def _():
        lax.fori_loop(0, nsub, x_start, 0)
        lax.fori_loop(0, nsub, x_wait, 0)

    def hidden(r):
        gu = jnp.dot(xv_ref[rows_of(r), :], wgu_ref[...], preferred_element_type=F32)
        g = gu[:, :tc]
        return (g * jax.nn.sigmoid(g) * gu[:, tc:]).astype(BF16)

    def y_copy(sub, slot):
        return pltpu.make_async_copy(
            stage_ref.at[slot],
            y_ref.at[pl.ds(pl.multiple_of(sub * tr * slabs, tr * slabs), tr * slabs), :],
            sem.at[1 + slot])

    def first_apply(r, slot, part):
        acc_ref[rows_of(r), :] = part

    def middle_apply(r, slot, part):
        acc_ref[rows_of(r), :] += part

    def final_apply(r, slot, part):
        res = acc_ref[rows_of(r), :] + part

        @pl.when(r >= 2)
        def _():
            y_copy(sub0 + r - 2, slot).wait()

        for f in range(slabs):
            stage_ref[slot, pl.ds(f, tr, stride=slabs), :] = res[:, f * LANES:(f + 1) * LANES]
        y_copy(sub0 + r, slot).start()

    def for_each_subtile(apply, group):
        def run(first, count):
            hids = [hidden(first + t) for t in range(count)]
            for t in range(count):
                part = jnp.dot(hids[t], wdn_ref[...], preferred_element_type=F32)
                apply(first + t, t % 2, part)

        def body(q, carry):
            run(group * q, group)
            return carry

        lax.fori_loop(0, nsub // group, body, 0)
        rest = nsub % group
        size = group // 2
        while size >= 1:
            @pl.when((rest // size) % 2 == 1)
            def _(size=size):
                run(nsub - rest % (2 * size), size)
            size //= 2

    @pl.when(live)
    def _():
        wgu_ref[:, :tc] = wg_ref[...].astype(BF16)
        wgu_ref[:, tc:] = wu_ref[...].astype(BF16)
        wdn_ref[...] = wd_ref[...].astype(BF16)

        @pl.when(c == 0)
        def _():
            for_each_subtile(first_apply, 2)

        @pl.when(jnp.logical_and(c > 0, c < last))
        def _():
            for_each_subtile(middle_apply, 4)

        @pl.when(c == last)
        def _():
            for_each_subtile(final_apply, 2)

            @pl.when(nsub >= 2)
            def _():
                y_copy(sub0 + nsub - 2, nsub % 2).wait()

            y_copy(sub0 + nsub - 1, (nsub - 1) % 2).wait()

    @pl.when(jnp.logical_and(p == pl.num_programs(0) - 1, c == last))
    def _():
        stage_ref[0] = jnp.zeros(stage_ref.shape[1:], F32)

        def zero(s, carry):
            y_copy(s, 0).start()
            y_copy(s, 0).wait()
            return carry

        lax.fori_loop(used_ref[0], y_ref.shape[0] // (tr * slabs), zero, 0)


def _expert_ffn(xs, pass_expert, pass_sub0, pass_nsub, used, w_gu, w_down, layer, tr, cap_sub, tc):
    rows, d = xs.shape
    fe = w_down.shape[2]
    nc = fe // tc
    assert nc >= 2
    slabs = d // LANES
    last = nc - 1

    def chunk(c, pn, p):
        live = jnp.minimum(pn[p], 1)
        return c * live + last * (1 - live)

    grid_spec = pltpu.PrefetchScalarGridSpec(
        num_scalar_prefetch=4,
        grid=(pass_expert.shape[0], nc),
        in_specs=[
            pl.BlockSpec(memory_space=pl.ANY),
            pl.BlockSpec((None, None, d, tc),
                         lambda p, c, pe, ps, pn, us: (layer, pe[p], 0, chunk(c, pn, p))),
            pl.BlockSpec((None, None, d, tc),
                         lambda p, c, pe, ps, pn, us: (layer, pe[p], 0, nc + chunk(c, pn, p))),
            pl.BlockSpec((None, None, tc, d),
                         lambda p, c, pe, ps, pn, us: (layer, pe[p], chunk(c, pn, p), 0)),
        ],
        out_specs=pl.BlockSpec(memory_space=pl.ANY),
        scratch_shapes=[pltpu.VMEM((cap_sub * tr, d), BF16), pltpu.VMEM((cap_sub * tr, d), F32),
                        pltpu.VMEM((d, 2 * tc), BF16), pltpu.VMEM((tc, d), BF16),
                        pltpu.VMEM((2, tr * slabs, LANES), F32), pltpu.SemaphoreType.DMA((3,))],
    )
    return pl.pallas_call(
        functools.partial(_expert_ffn_kernel, tr=tr, slabs=slabs),
        grid_spec=grid_spec,
        out_shape=jax.ShapeDtypeStruct((rows * slabs, LANES), F32),
        compiler_params=_params(("arbitrary", "arbitrary")),
        name="moe_expert_swiglu",
    )(pass_expert, pass_sub0, pass_nsub, used, xs, w_gu, w_gu, w_down)


def _combine_kernel(s1_ref, s2_ref, y_ref, rw_ref, x_ref, g_ref, o_ref, b1_ref, b2_ref, sem,
                    *, slabs):
    i = pl.program_id(0)
    last = pl.num_programs(0) - 1
    tm = x_ref.shape[0]
    slot = i % 2

    def fetch(tile, s, wait):
        _gather_rows(y_ref, b1_ref.at[s], sem.at[0, s], s1_ref, tile * tm, tm, slabs, wait)
        _gather_rows(y_ref, b2_ref.at[s], sem.at[1, s], s2_ref, tile * tm, tm, slabs, wait)

    @pl.when(i == 0)
    def _():
        fetch(0, 0, False)

    @pl.when(i < last)
    def _():
        fetch(i + 1, 1 - slot, False)

    fetch(i, slot, True)

    rw = rw_ref[...]
    w1 = rw[:, 0:1]
    w2 = rw[:, 1:2]
    parts = []
    ssq = jnp.zeros((tm, 1), F32)
    for f in range(slabs):
        m = (w1 * b1_ref[slot, pl.ds(f, tm, stride=slabs), :]
             + w2 * b2_ref[slot, pl.ds(f, tm, stride=slabs), :])
        ssq = ssq + jnp.sum(m * m, axis=1, keepdims=True)
        parts.append(m)
    inv = lax.rsqrt(ssq / (slabs * LANES) + NORM_EPS)
    for f in range(slabs):
        sl = slice(f * LANES, (f + 1) * LANES)
        o_ref[:, sl] = x_ref[:, sl] + parts[f] * inv * g_ref[:, sl]


def _combine(slot1, slot2, y_slabs, rw, x, g, tm):
    n, d = x.shape
    slabs = d // LANES
    grid_spec = pltpu.PrefetchScalarGridSpec(
        num_scalar_prefetch=2,
        grid=(n // tm,),
        in_specs=[
            pl.BlockSpec(memory_space=pl.ANY),
            pl.BlockSpec((tm, LANES), lambda i, s1, s2: (i, 0)),
            pl.BlockSpec((tm, d), lambda i, s1, s2: (i, 0)),
            pl.BlockSpec((1, d), lambda i, s1, s2: (0, 0)),
        ],
        out_specs=pl.BlockSpec((tm, d), lambda i, s1, s2: (i, 0)),
        scratch_shapes=[pltpu.VMEM((2, tm * slabs, LANES), F32),
                        pltpu.VMEM((2, tm * slabs, LANES), F32),
                        pltpu.SemaphoreType.DMA((2, 2))],
    )
    return pl.pallas_call(
        functools.partial(_combine_kernel, slabs=slabs),
        grid_spec=grid_spec,
        out_shape=jax.ShapeDtypeStruct((n, d), F32),
        compiler_params=_params(("arbitrary",)),
        name="moe_combine",
    )(slot1, slot2, y_slabs, rw, x, g)


def _moe_ffn(x, g_in, g_out, w_router, w_gu, w_down, layer, tiles):
    n, d = x.shape
    experts = w_router.shape[1]
    tr = tiles["expert_rows"]
    cap_sub = tiles["expert_cap"] // tr
    h_slabs, ri, rw, cnt = _router(x, g_in, w_router, tiles["router_rows"])

    counts = cnt[0, :experts]
    nsub = (counts + tr - 1) // tr
    sub_end = jnp.cumsum(nsub)
    sub_start = sub_end - nsub
    slot1 = (sub_start[ri[:, 0]] * tr + ri[:, 2]).astype(jnp.int32)
    slot2 = (sub_start[ri[:, 1]] * tr + ri[:, 3]).astype(jnp.int32)
    rows = TOP_K * n + experts * tr
    n_sub = rows // tr
    token = jnp.arange(n, dtype=jnp.int32)
    tok_of_slot = jnp.zeros((rows,), jnp.int32).at[slot1].set(token).at[slot2].set(token)
    sub_live = (jnp.arange(n_sub, dtype=jnp.int32) < sub_end[-1]).astype(jnp.int32)

    n_pass = experts + n_sub // cap_sub
    passes = (nsub + cap_sub - 1) // cap_sub
    pass_end = jnp.cumsum(passes)
    pidx = jnp.arange(n_pass, dtype=jnp.int32)
    pe = jnp.minimum(jnp.searchsorted(pass_end, pidx, side="right"), experts - 1)
    local = pidx - (pass_end[pe] - passes[pe])
    p_live = pidx < pass_end[-1]
    pass_nsub = jnp.where(p_live, jnp.minimum(cap_sub, nsub[pe] - local * cap_sub), 0)
    pass_sub0 = jnp.where(p_live, sub_start[pe] + local * cap_sub, 0)
    last_pe = jnp.max(jnp.where(passes > 0, jnp.arange(experts), 0))
    pass_expert = jnp.where(p_live, pe, last_pe)

    xs = _dispatch(tok_of_slot, sub_live, h_slabs, d, tr)
    y_slabs = _expert_ffn(xs, pass_expert.astype(jnp.int32), pass_sub0.astype(jnp.int32),
                          pass_nsub.astype(jnp.int32), sub_end[-1:].astype(jnp.int32),
                          w_gu, w_down, layer, tr, cap_sub,
                          tiles["expert_chunk"])
    return _combine(slot1, slot2, y_slabs, rw, x, g_out, tiles["combine_rows"])


def _tiles(n, seq, d, ffn, expert_ffn):
    pick = lambda pref, dim: min(pref, dim)
    return {
        "pw1_rows": pick(512, n), "pw1_cols": pick(512, d),
        "conv_rows": pick(1024, seq), "conv_cols": pick(256, d),
        "ln_rows": pick(256, n),
        "ffn_rows": pick(512, n), "ffn_chunk": pick(512, ffn),
        "proj_rows": pick(256, seq),
        "attn_q": pick(512, seq), "attn_k": pick(512, seq),
        "out_rows": pick(256, n),
        "router_rows": pick(512, n),
        "expert_rows": pick(256, n), "expert_cap": pick(2304, TOP_K * n),
        "expert_chunk": pick(256, expert_ffn),
        "combine_rows": pick(256, n),
    }


def _rope_tables(seq):
    pos = jnp.arange(seq, dtype=F32)
    inv_freq = ROPE_THETA ** (-jnp.arange(0, QK_ROPE_DIM, 2, dtype=F32) / QK_ROPE_DIM)
    ang = pos[:, None] * inv_freq[None, :]
    cos, sin = jnp.cos(ang), jnp.sin(ang)
    zeros = jnp.zeros((seq, LANES - QK_ROPE_DIM), F32)
    return (jnp.concatenate([cos, cos, zeros], axis=1),
            jnp.concatenate([-sin, sin, zeros], axis=1))


def _swap_halves(w):
    half = w.shape[-1] // 2
    return jnp.concatenate([w[..., half:], w[..., :half]], axis=-1)


def kernel(x, norm_g, conv_w_pw1, conv_b_pw1, conv_w_dw, conv_b_dw, conv_ln_g, conv_ln_b,
           conv_w_pw2, ffn_w_gu, ffn_w_down, moe_w_router, moe_w_gu, moe_w_down,
           kv_in_g, kv_w_a, kv_latent_g, kv_w_b, attn_w_q_a, attn_q_latent_g,
           attn_w_q_b, attn_w_o):
    batch, seq, d = x.shape
    n = batch * seq
    depth = norm_g.shape[0]
    n_conv = depth // 2
    heads = d // 128
    kv_rank = kv_latent_g.shape[0]
    tiles = _tiles(n, seq, d, ffn_w_down.shape[1], moe_w_down.shape[2])
    cos2, sin2 = _rope_tables(seq)
    vec = lambda v: v.reshape(1, -1)

    xs = x.reshape(n, d)
    k = v = None
    for i in range(depth):
        if i == n_conv:
            rope_w = kv_w_a[:, kv_rank:]
            wa_ext = jnp.concatenate([kv_w_a, _swap_halves(rope_w)], axis=1).astype(BF16)
            wb = kv_w_b.reshape(kv_rank, heads, QK_NOPE_DIM + V_DIM)
            wb_split = jnp.concatenate(
                [wb[:, :, :QK_NOPE_DIM].reshape(kv_rank, heads * QK_NOPE_DIM),
                 wb[:, :, QK_NOPE_DIM:].reshape(kv_rank, heads * V_DIM)], axis=1).astype(BF16)
            k, v = _kv_proj(xs, vec(kv_in_g), wa_ext, vec(kv_latent_g), wb_split, cos2, sin2,
                            seq, tiles["proj_rows"])
        if i < n_conv:
            u = _pw1_glu(xs, vec(norm_g[i, 0]), conv_w_pw1[i].astype(BF16), vec(conv_b_pw1[i]),
                         tiles["pw1_rows"], tiles["pw1_cols"])
            c = _dwconv(u, conv_w_dw[i], vec(conv_b_dw[i]), seq, tiles["conv_rows"],
                        tiles["conv_cols"])
            xs = _ln_pw2(c, xs, vec(conv_ln_g[i]), vec(conv_ln_b[i]), conv_w_pw2[i].astype(BF16),
                         vec(norm_g[i, 1]), tiles["ln_rows"])
        else:
            j = i - n_conv
            q_rank = attn_w_q_a.shape[2]
            wqb = attn_w_q_b[j].reshape(q_rank, heads, QK_NOPE_DIM + QK_ROPE_DIM)
            rope_w = wqb[:, :, QK_NOPE_DIM:]
            wqb_ext = jnp.concatenate([wqb, _swap_halves(rope_w)], axis=2)
            wqb_ext = wqb_ext.reshape(q_rank, heads * 256).astype(BF16)
            q = _q_proj(xs, vec(norm_g[i, 0]), attn_w_q_a[j].astype(BF16),
                        vec(attn_q_latent_g[j]), wqb_ext, cos2, sin2, seq, tiles["proj_rows"])
            o = _attention(q, k, v, batch, seq, tiles["attn_q"], tiles["attn_k"])
            xs = _attn_out(o, xs, attn_w_o[j].astype(BF16), vec(norm_g[i, 1]), tiles["out_rows"])
        if i % 2 == 0:
            xs = _dense_ffn(xs, vec(norm_g[i, 2]), vec(norm_g[i, 3]), ffn_w_gu[i // 2].astype(BF16),
                            ffn_w_down[i // 2].astype(BF16), tiles["ffn_rows"], tiles["ffn_chunk"])
        else:
            xs = _moe_ffn(xs, vec(norm_g[i, 2]), vec(norm_g[i, 3]), moe_w_router[i // 2],
                          moe_w_gu, moe_w_down, i // 2, tiles)
    return xs.reshape(batch, seq, d)
```

```python
import functools

import jax
import jax.numpy as jnp
from jax import lax
from jax.experimental import pallas as pl
from jax.experimental.pallas import tpu as pltpu

NORM_EPS = 1e-6
ROPE_THETA = 10000.0
QK_NOPE_DIM = 128
QK_ROPE_DIM = 64
V_DIM = 128
TOP_K = 2

LANES = 128
SUBLANES = 8
CONV_HALO = 32
GATHER_UNROLL = 8
VMEM_LIMIT = 56 * 1024 * 1024

F32 = jnp.float32
BF16 = jnp.bfloat16


def _params(semantics):
    return pltpu.CompilerParams(dimension_semantics=semantics, vmem_limit_bytes=VMEM_LIMIT)


def _rms(x):
    return x * lax.rsqrt(jnp.mean(x * x, axis=-1, keepdims=True) + NORM_EPS)


def _pw1_glu_kernel(x_ref, g_ref, wa_ref, wg_ref, ba_ref, bg_ref, o_ref, xn_ref):
    @pl.when(pl.program_id(1) == 0)
    def _():
        xn_ref[...] = (_rms(x_ref[...]) * g_ref[...]).astype(BF16)

    xn = xn_ref[...]
    a = jnp.dot(xn, wa_ref[...], preferred_element_type=F32) + ba_ref[...]
    g = jnp.dot(xn, wg_ref[...], preferred_element_type=F32) + bg_ref[...]
    o_ref[...] = (a * jax.nn.sigmoid(g)).astype(o_ref.dtype)


def _pw1_glu(x, g, w, b, tm, tn):
    n, d = x.shape
    nj = d // tn
    return pl.pallas_call(
        _pw1_glu_kernel,
        grid=(n // tm, nj),
        in_specs=[
            pl.BlockSpec((tm, d), lambda i, j: (i, 0)),
            pl.BlockSpec((1, d), lambda i, j: (0, 0)),
            pl.BlockSpec((d, tn), lambda i, j: (0, j)),
            pl.BlockSpec((d, tn), lambda i, j: (0, nj + j)),
            pl.BlockSpec((1, tn), lambda i, j: (0, j)),
            pl.BlockSpec((1, tn), lambda i, j: (0, nj + j)),
        ],
        out_specs=pl.BlockSpec((tm, tn), lambda i, j: (i, j)),
        out_shape=jax.ShapeDtypeStruct((n, d), BF16),
        scratch_shapes=[pltpu.VMEM((tm, d), BF16)],
        compiler_params=_params(("parallel", "arbitrary")),
        name="conv_pw1_glu",
    )(x, g, w, w, b, b)


def _dwconv_kernel(u_ref, halo_ref, w_ref, b_ref, o_ref, sh_ref, *, tiles_per_seq, width, rb):
    tm = u_ref.shape[0]
    ext = tm + CONV_HALO
    first = (pl.program_id(0) % tiles_per_seq) == 0
    halo = halo_ref[...].astype(F32)
    sh_ref[0, 0:CONV_HALO, :] = jnp.where(first, jnp.zeros_like(halo), halo)
    sh_ref[0, CONV_HALO:ext, :] = u_ref[...].astype(F32)
    for j in range(1, SUBLANES):
        sh_ref[j, 0:ext - SUBLANES, :] = sh_ref[0, j:ext - SUBLANES + j, :]
    base = CONV_HALO - (width - 1)

    def block(blk, carry):
        r0 = pl.multiple_of(blk * rb, rb)
        acc = None
        for k in range(width):
            j = (base + k) % SUBLANES
            a = (base + k) - j
            term = w_ref[k:k + 1, :] * sh_ref[j, pl.ds(r0 + a, rb), :]
            acc = term if acc is None else acc + term
        o_ref[pl.ds(r0, rb), :] = (acc + b_ref[...]).astype(o_ref.dtype)
        return carry

    lax.fori_loop(0, tm // rb, block, 0)


def _dwconv(u, w, b, seq, tm, tc):
    n, d = u.shape
    width = w.shape[0]
    hb = tm // CONV_HALO
    kern = functools.partial(_dwconv_kernel, tiles_per_seq=seq // tm, width=width, rb=min(64, tm))
    return pl.pallas_call(
        kern,
        grid=(n // tm, d // tc),
        in_specs=[
            pl.BlockSpec((tm, tc), lambda i, j: (i, j)),
            pl.BlockSpec((CONV_HALO, tc), lambda i, j: (jnp.maximum(i * hb - 1, 0), j)),
            pl.BlockSpec((width, tc), lambda i, j: (0, j)),
            pl.BlockSpec((1, tc), lambda i, j: (0, j)),
        ],
        out_specs=pl.BlockSpec((tm, tc), lambda i, j: (i, j)),
        out_shape=jax.ShapeDtypeStruct((n, d), BF16),
        scratch_shapes=[pltpu.VMEM((SUBLANES, tm + CONV_HALO, tc), F32)],
        compiler_params=_params(("parallel", "parallel")),
        name="conv_depthwise",
    )(u, u, w, b)


def _ln_pw2_kernel(c_ref, x_ref, lg_ref, lb_ref, w_ref, g_ref, o_ref):
    c = c_ref[...].astype(F32)
    mu = jnp.mean(c, axis=-1, keepdims=True)
    cc = c - mu
    y = cc * lax.rsqrt(jnp.mean(cc * cc, axis=-1, keepdims=True) + NORM_EPS)
    y = y * lg_ref[...] + lb_ref[...]
    y = (y * jax.nn.sigmoid(y)).astype(BF16)
    m = jnp.dot(y, w_ref[...], preferred_element_type=F32)
    o_ref[...] = x_ref[...] + _rms(m) * g_ref[...]


def _ln_pw2(c, x, ln_g, ln_b, w, g, tm):
    n, d = x.shape
    row = pl.BlockSpec((tm, d), lambda i: (i, 0))
    vec = pl.BlockSpec((1, d), lambda i: (0, 0))
    return pl.pallas_call(
        _ln_pw2_kernel,
        grid=(n // tm,),
        in_specs=[row, row, vec, vec, pl.BlockSpec((d, d), lambda i: (0, 0)), vec],
        out_specs=row,
        out_shape=jax.ShapeDtypeStruct((n, d), F32),
        compiler_params=_params(("parallel",)),
        name="conv_ln_pw2",
    )(c, x, ln_g, ln_b, w, g)


def _swiglu_chunk(xn_ref, wg_ref, wu_ref, wd_ref, acc_ref, groups):
    rows = xn_ref.shape[0] // groups
    hids = []
    for i in range(groups):
        xn = xn_ref[i * rows:(i + 1) * rows, :]
        g = jnp.dot(xn, wg_ref[...], preferred_element_type=F32)
        u = jnp.dot(xn, wu_ref[...], preferred_element_type=F32)
        hids.append((g * jax.nn.sigmoid(g) * u).astype(BF16))
    for i in range(groups):
        acc_ref[i * rows:(i + 1) * rows, :] += jnp.dot(hids[i], wd_ref[...],
                                                       preferred_element_type=F32)


def _dense_ffn_kernel(x_ref, gi_ref, go_ref, wg_ref, wu_ref, wd_ref, o_ref, xn_ref, acc_ref):
    c = pl.program_id(1)

    @pl.when(c == 0)
    def _():
        xn_ref[...] = (_rms(x_ref[...]) * gi_ref[...]).astype(BF16)
        acc_ref[...] = jnp.zeros_like(acc_ref)

    _swiglu_chunk(xn_ref, wg_ref, wu_ref, wd_ref, acc_ref, groups=2)

    @pl.when(c == pl.num_programs(1) - 1)
    def _():
        o_ref[...] = x_ref[...] + _rms(acc_ref[...]) * go_ref[...]


def _dense_ffn(x, g_in, g_out, w_gu, w_down, tm, tc):
    n, d = x.shape
    f = w_down.shape[0]
    nc = f // tc
    row = pl.BlockSpec((tm, d), lambda i, c: (i, 0))
    vec = pl.BlockSpec((1, d), lambda i, c: (0, 0))
    return pl.pallas_call(
        _dense_ffn_kernel,
        grid=(n // tm, nc),
        in_specs=[
            row, vec, vec,
            pl.BlockSpec((d, tc), lambda i, c: (0, c)),
            pl.BlockSpec((d, tc), lambda i, c: (0, nc + c)),
            pl.BlockSpec((tc, d), lambda i, c: (c, 0)),
        ],
        out_specs=row,
        out_shape=jax.ShapeDtypeStruct((n, d), F32),
        scratch_shapes=[pltpu.VMEM((tm, d), BF16), pltpu.VMEM((tm, d), F32)],
        compiler_params=_params(("parallel", "arbitrary")),
        name="dense_swiglu",
    )(x, g_in, g_out, w_gu, w_gu, w_down)


def _rope128(t, cos2, sin2):
    return t * cos2 + pltpu.roll(t, 64, axis=1) * sin2


def _kv_proj_kernel(x_ref, g_ref, wa_ref, lg_ref, wb_ref, cos_ref, sin_ref, k_ref, v_ref,
                    *, heads, rank):
    h = (_rms(x_ref[...]) * g_ref[...]).astype(BF16)
    a = jnp.dot(h, wa_ref[...], preferred_element_type=F32)
    c_kv = (_rms(a[:, :rank]) * lg_ref[...]).astype(BF16)
    k_rope = _rope128(a[:, rank:], cos_ref[...], sin_ref[...]).astype(BF16)
    kv = jnp.dot(c_kv, wb_ref[...], preferred_element_type=F32)
    hd = heads * QK_NOPE_DIM
    for hh in range(heads):
        k_ref[:, hh * 256:hh * 256 + 128] = kv[:, hh * 128:(hh + 1) * 128].astype(BF16)
        k_ref[:, hh * 256 + 128:(hh + 1) * 256] = k_rope
    v_ref[...] = kv[:, hd:].astype(BF16)


def _kv_proj(x, g, wa_ext, lat_g, wb_split, cos2, sin2, seq, tm):
    n, d = x.shape
    heads = d // 128
    rank = lat_g.shape[1]
    spt = seq // tm
    kern = functools.partial(_kv_proj_kernel, heads=heads, rank=rank)
    full = lambda a: pl.BlockSpec(a.shape, lambda i: (0, 0))
    return pl.pallas_call(
        kern,
        grid=(n // tm,),
        in_specs=[
            pl.BlockSpec((tm, d), lambda i: (i, 0)), full(g), full(wa_ext), full(lat_g),
            full(wb_split),
            pl.BlockSpec((tm, LANES), lambda i: (i % spt, 0)),
            pl.BlockSpec((tm, LANES), lambda i: (i % spt, 0)),
        ],
        out_specs=[pl.BlockSpec((tm, heads * 256), lambda i: (i, 0)),
                   pl.BlockSpec((tm, heads * 128), lambda i: (i, 0))],
        out_shape=[jax.ShapeDtypeStruct((n, heads * 256), BF16),
                   jax.ShapeDtypeStruct((n, heads * 128), BF16)],
        compiler_params=_params(("parallel",)),
        name="mla_kv_proj",
    )(x, g, wa_ext, lat_g, wb_split, cos2, sin2)


def _q_proj_kernel(x_ref, g_ref, wa_ref, lg_ref, wb_ref, cos_ref, sin_ref, q_ref, *, heads, scale):
    h = (_rms(x_ref[...]) * g_ref[...]).astype(BF16)
    lat = jnp.dot(h, wa_ref[...], preferred_element_type=F32)
    lat = (_rms(lat) * lg_ref[...]).astype(BF16)
    q = jnp.dot(lat, wb_ref[...], preferred_element_type=F32)
    cos2 = cos_ref[...]
    sin2 = sin_ref[...]
    for hh in range(heads):
        q_ref[:, hh * 256:hh * 256 + 128] = (q[:, hh * 256:hh * 256 + 128] * scale).astype(BF16)
        r = _rope128(q[:, hh * 256 + 128:(hh + 1) * 256], cos2, sin2)
        q_ref[:, hh * 256 + 128:(hh + 1) * 256] = (r * scale).astype(BF16)


def _q_proj(x, g, wa, lat_g, wb_ext, cos2, sin2, seq, tm):
    n, d = x.shape
    heads = d // 128
    spt = seq // tm
    scale = float(QK_NOPE_DIM + QK_ROPE_DIM) ** -0.5 * 1.4426950408889634
    kern = functools.partial(_q_proj_kernel, heads=heads, scale=scale)
    full = lambda a: pl.BlockSpec(a.shape, lambda i: (0, 0))
    return pl.pallas_call(
        kern,
        grid=(n // tm,),
        in_specs=[
            pl.BlockSpec((tm, d), lambda i: (i, 0)), full(g), full(wa), full(lat_g), full(wb_ext),
            pl.BlockSpec((tm, LANES), lambda i: (i % spt, 0)),
            pl.BlockSpec((tm, LANES), lambda i: (i % spt, 0)),
        ],
        out_specs=pl.BlockSpec((tm, heads * 256), lambda i: (i, 0)),
        out_shape=jax.ShapeDtypeStruct((n, heads * 256), BF16),
        compiler_params=_params(("parallel",)),
        name="mla_q_proj",
    )(x, g, wa, lat_g, wb_ext, cos2, sin2)


def _attn_kernel(q_ref, k_ref, v_ref, o_ref, *, tq, tk):
    seq = q_ref.shape[0]
    row = lax.broadcasted_iota(jnp.int32, (tq, tk), 0)
    col = lax.broadcasted_iota(jnp.int32, (tq, tk), 1)
    for qi in range(seq // tq):
        q = q_ref[qi * tq:(qi + 1) * tq, :]
        m = jnp.full((tq, 1), -jnp.inf, F32)
        l = jnp.zeros((tq, 1), F32)
        acc = jnp.zeros((tq, V_DIM), F32)
        for j in range((qi + 1) * tq // tk):
            k = k_ref[j * tk:(j + 1) * tk, :]
            s = lax.dot_general(q, k, (((1,), (1,)), ((), ())), preferred_element_type=F32)
            if (j + 1) * tk > qi * tq + 1:
                s = jnp.where(row + qi * tq >= col + j * tk, s, -jnp.inf)
            m_new = jnp.maximum(m, jnp.max(s, axis=1, keepdims=True))
            p = jnp.exp2(s - m_new)
            alpha = jnp.exp2(m - m_new)
            l = alpha * l + jnp.sum(p, axis=1, keepdims=True)
            pv = jnp.dot(p.astype(BF16), v_ref[j * tk:(j + 1) * tk, :],
                         preferred_element_type=F32)
            acc = alpha * acc + pv
            m = m_new
        o_ref[qi * tq:(qi + 1) * tq, :] = (acc / l).astype(o_ref.dtype)


def _attention(q, k, v, batch, seq, tq, tk):
    n = q.shape[0]
    heads = v.shape[1] // V_DIM
    kern = functools.partial(_attn_kernel, tq=tq, tk=tk)
    return pl.pallas_call(
        kern,
        grid=(batch, heads),
        in_specs=[
            pl.BlockSpec((seq, 256), lambda b, h: (b, h)),
            pl.BlockSpec((seq, 256), lambda b, h: (b, h)),
            pl.BlockSpec((seq, V_DIM), lambda b, h: (b, h)),
        ],
        out_specs=pl.BlockSpec((seq, V_DIM), lambda b, h: (b, h)),
        out_shape=jax.ShapeDtypeStruct((n, heads * V_DIM), BF16),
        compiler_params=_params(("parallel", "parallel")),
        name="mla_attention",
    )(q, k, v)


def _attn_out_kernel(o_ref, x_ref, w_ref, g_ref, xo_ref):
    m = jnp.dot(o_ref[...], w_ref[...], preferred_element_type=F32)
    xo_ref[...] = x_ref[...] + _rms(m) * g_ref[...]


def _attn_out(o, x, w, g, tm):
    n, d = x.shape
    row = pl.BlockSpec((tm, d), lambda i: (i, 0))
    return pl.pallas_call(
        _attn_out_kernel,
        grid=(n // tm,),
        in_specs=[pl.BlockSpec((tm, o.shape[1]), lambda i: (i, 0)), row,
                  pl.BlockSpec(w.shape, lambda i: (0, 0)), pl.BlockSpec((1, d), lambda i: (0, 0))],
        out_specs=row,
        out_shape=jax.ShapeDtypeStruct((n, d), F32),
        compiler_params=_params(("parallel",)),
        name="mla_out_proj",
    )(o, x, w, g)


def _router_kernel(x_ref, g_ref, wcat_ref, h_ref, ri_ref, rw_ref, cnt_ref, base_ref,
                   *, experts, slabs):
    tm = x_ref.shape[0]

    @pl.when(pl.program_id(0) == 0)
    def _():
        base_ref[...] = jnp.zeros_like(base_ref)

    h = _rms(x_ref[...]) * g_ref[...]
    for f in range(slabs):
        h_ref[pl.ds(f, tm, stride=slabs), :] = h[:, f * LANES:(f + 1) * LANES]

    h_hi = h.astype(BF16)
    h_lo = (h - h_hi.astype(F32)).astype(BF16)
    hi = jnp.dot(h_hi, wcat_ref[...], preferred_element_type=F32)
    logits = (hi[:, :LANES] + hi[:, LANES:]
              + jnp.dot(h_lo, wcat_ref[:, :LANES], preferred_element_type=F32))

    lane = lax.broadcasted_iota(jnp.int32, (tm, LANES), 1)
    lg = jnp.where(lane < experts, logits, -jnp.inf)
    m1 = jnp.max(lg, axis=1, keepdims=True)
    i1 = jnp.min(jnp.where(lg == m1, lane, LANES), axis=1, keepdims=True)
    oh1 = lane == i1
    lg2 = jnp.where(oh1, -jnp.inf, lg)
    m2 = jnp.max(lg2, axis=1, keepdims=True)
    i2 = jnp.min(jnp.where(lg2 == m2, lane, LANES), axis=1, keepdims=True)
    oh2 = lane == i2
    e2 = jnp.exp(m2 - m1)
    w1 = 1.0 / (1.0 + e2)
    w2 = e2 / (1.0 + e2)

    cnt = oh1.astype(F32) + oh2.astype(F32)
    r_i = lax.broadcasted_iota(jnp.int32, (tm, tm), 0)
    c_i = lax.broadcasted_iota(jnp.int32, (tm, tm), 1)
    tri = (c_i < r_i).astype(BF16)
    before = jnp.dot(tri, cnt.astype(BF16), preferred_element_type=F32) + base_ref[0:1, :]
    rank1 = jnp.sum(jnp.where(oh1, before, 0.0), axis=1, keepdims=True).astype(jnp.int32)
    rank2 = jnp.sum(jnp.where(oh2, before, 0.0), axis=1, keepdims=True).astype(jnp.int32)
    total = base_ref[0:1, :] + jnp.sum(cnt, axis=0, keepdims=True)
    base_ref[...] = jnp.broadcast_to(total, base_ref.shape)
    cnt_ref[...] = jnp.broadcast_to(total, cnt_ref.shape).astype(jnp.int32)

    ri_ref[...] = jnp.where(lane == 0, i1, jnp.where(lane == 1, i2,
                            jnp.where(lane == 2, rank1, jnp.where(lane == 3, rank2, 0))))
    rw_ref[...] = jnp.where(lane == 0, w1, jnp.where(lane == 1, w2, 0.0))


def _router(x, g, w_router, tm):
    n, d = x.shape
    experts = w_router.shape[1]
    slabs = d // LANES
    w_pad = jnp.zeros((d, LANES), F32).at[:, :experts].set(w_router)
    w_hi = w_pad.astype(BF16)
    w_lo = (w_pad - w_hi.astype(F32)).astype(BF16)
    w_cat = jnp.concatenate([w_hi, w_lo], axis=1)
    kern = functools.partial(_router_kernel, experts=experts, slabs=slabs)
    return pl.pallas_call(
        kern,
        grid=(n // tm,),
        in_specs=[pl.BlockSpec((tm, d), lambda i: (i, 0)), pl.BlockSpec((1, d), lambda i: (0, 0)),
                  pl.BlockSpec((d, 2 * LANES), lambda i: (0, 0))],
        out_specs=[pl.BlockSpec((tm * slabs, LANES), lambda i: (i, 0)),
                   pl.BlockSpec((tm, LANES), lambda i: (i, 0)),
                   pl.BlockSpec((tm, LANES), lambda i: (i, 0)),
                   pl.BlockSpec((8, LANES), lambda i: (0, 0))],
        out_shape=[jax.ShapeDtypeStruct((n * slabs, LANES), F32),
                   jax.ShapeDtypeStruct((n, LANES), jnp.int32),
                   jax.ShapeDtypeStruct((n, LANES), F32),
                   jax.ShapeDtypeStruct((8, LANES), jnp.int32)],
        scratch_shapes=[pltpu.VMEM((8, LANES), F32)],
        compiler_params=_params(("arbitrary",)),
        name="moe_router",
    )(x, g, w_cat)


def _pitch(slabs):
    groups = slabs // SUBLANES
    return slabs if groups % 2 == 1 else slabs + SUBLANES


def _row_copy(src_ref, dst_ref, sem, src_row, dst_row, slabs):
    pitch = _pitch(slabs)
    return pltpu.make_async_copy(
        src_ref.at[pl.ds(pl.multiple_of(src_row * slabs, slabs), slabs), :],
        dst_ref.at[pl.ds(pl.multiple_of(dst_row * pitch, SUBLANES), slabs), :],
        sem)


def _gather_rows(src_ref, dst_ref, sem, idx_ref, base, count, slabs, wait):
    def body(q, carry):
        for j in range(GATHER_UNROLL):
            r = q * GATHER_UNROLL + j
            cp = _row_copy(src_ref, dst_ref, sem, idx_ref[base + r], r, slabs)
            if wait:
                cp.wait()
            else:
                cp.start(priority=j % 2)
        return carry

    assert count % GATHER_UNROLL == 0
    lax.fori_loop(0, count // GATHER_UNROLL, body, 0)


def _dispatch_kernel(tok_ref, live_ref, h_ref, o_ref, gbuf_ref, sem, *, slabs):
    i = pl.program_id(0)
    last = pl.num_programs(0) - 1
    tr = o_ref.shape[0]
    slot = i % 2
    nxt = jnp.minimum(i + 1, last)

    @pl.when(jnp.logical_and(i == 0, live_ref[0] == 1))
    def _():
        _gather_rows(h_ref, gbuf_ref.at[0], sem.at[0], tok_ref, 0, tr, slabs, wait=False)

    @pl.when(jnp.logical_and(i < last, live_ref[nxt] == 1))
    def _():
        _gather_rows(h_ref, gbuf_ref.at[1 - slot], sem.at[1 - slot], tok_ref, nxt * tr, tr, slabs,
                     wait=False)

    @pl.when(live_ref[i] == 1)
    def _():
        _gather_rows(h_ref, gbuf_ref.at[slot], sem.at[slot], tok_ref, i * tr, tr, slabs, wait=True)
        for f in range(slabs):
            o_ref[:, f * LANES:(f + 1) * LANES] = (
                gbuf_ref[slot, pl.ds(f, tr, stride=_pitch(slabs)), :].astype(BF16))

    @pl.when(live_ref[i] == 0)
    def _():
        o_ref[...] = jnp.zeros_like(o_ref)


def _dispatch(tok_of_slot, sub_live, h_slabs, d, tr):
    rows = tok_of_slot.shape[0]
    slabs = d // LANES
    grid_spec = pltpu.PrefetchScalarGridSpec(
        num_scalar_prefetch=2,
        grid=(rows // tr,),
        in_specs=[pl.BlockSpec(memory_space=pl.ANY)],
        out_specs=pl.BlockSpec((tr, d), lambda i, tok, live: (i, 0)),
        scratch_shapes=[pltpu.VMEM((2, tr * _pitch(slabs), LANES), F32),
                        pltpu.SemaphoreType.DMA((2,))],
    )
    return pl.pallas_call(
        functools.partial(_dispatch_kernel, slabs=slabs),
        grid_spec=grid_spec,
        out_shape=jax.ShapeDtypeStruct((rows, d), BF16),
        compiler_params=_params(("arbitrary",)),
        name="moe_dispatch",
    )(tok_of_slot, sub_live, h_slabs)


def _expert_ffn_kernel(pe_ref, ps_ref, pn_ref, used_ref, xs_ref, wg_ref, wu_ref, wd_ref, y_ref,
                       xv_ref, acc_ref, wgu_ref, wdn_ref, stage_ref, sem, *, tr, slabs):
    p = pl.program_id(0)
    c = pl.program_id(1)
    last = pl.num_programs(1) - 1
    tc = wd_ref.shape[0]
    nsub = pn_ref[p]
    sub0 = ps_ref[p]
    live = nsub > 0

    def rows_of(r):
        return pl.ds(pl.multiple_of(r * tr, tr), tr)

    def x_copy(r):
        return pltpu.make_async_copy(
            xs_ref.at[pl.ds(pl.multiple_of((sub0 + r) * tr, tr), tr), :],
            xv_ref.at[rows_of(r), :], sem.at[0])

    def x_start(r, carry):
        x_copy(r).start()
        return carry

    def x_wait(r, carry):
        x_copy(r).wait()
        return carry

    @pl.when(jnp.logical_and(live, c == 0))
    def _():
        lax.fori_loop(0, nsub, x_start, 0)
        lax.fori_loop(0, nsub, x_wait, 0)

    def hidden(r):
        gu = jnp.dot(xv_ref[rows_of(r), :], wgu_ref[...], preferred_element_type=F32)
        g = gu[:, :tc]
        return (g * jax.nn.sigmoid(g) * gu[:, tc:]).astype(BF16)

    def y_copy(sub, slot):
        return pltpu.make_async_copy(
            stage_ref.at[slot],
            y_ref.at[pl.ds(pl.multiple_of(sub * tr * slabs, tr * slabs), tr * slabs), :],
            sem.at[1 + slot])

    def first_apply(r, slot, part):
        acc_ref[rows_of(r), :] = part

    def middle_apply(r, slot, part):
        acc_ref[rows_of(r), :] += part

    def final_apply(r, slot, part):
        res = acc_ref[rows_of(r), :] + part

        @pl.when(r >= 2)
        def _():
            y_copy(sub0 + r - 2, slot).wait()

        for f in range(slabs):
            stage_ref[slot, pl.ds(f, tr, stride=slabs), :] = res[:, f * LANES:(f + 1) * LANES]
        y_copy(sub0 + r, slot).start()

    def for_each_subtile(apply, group):
        def run(first, count):
            hids = [hidden(first + t) for t in range(count)]
            for t in range(count):
                part = jnp.dot(hids[t], wdn_ref[...], preferred_element_type=F32)
                apply(first + t, t % 2, part)

        def body(q, carry):
            run(group * q, group)
            return carry

        lax.fori_loop(0, nsub // group, body, 0)
        rest = nsub % group
        size = group // 2
        while size >= 1:
            @pl.when((rest // size) % 2 == 1)
            def _(size=size):
                run(nsub - rest % (2 * size), size)
            size //= 2

    @pl.when(live)
    def _():
        wgu_ref[:, :tc] = wg_ref[...].astype(BF16)
        wgu_ref[:, tc:] = wu_ref[...].astype(BF16)
        wdn_ref[...] = wd_ref[...].astype(BF16)

        @pl.when(c == 0)
        def _():
            for_each_subtile(first_apply, 2)

        @pl.when(jnp.logical_and(c > 0, c < last))
        def _():
            for_each_subtile(middle_apply, 4)

        @pl.when(c == last)
        def _():
            for_each_subtile(final_apply, 2)

            @pl.when(nsub >= 2)
            def _():
                y_copy(sub0 + nsub - 2, nsub % 2).wait()

            y_copy(sub0 + nsub - 1, (nsub - 1) % 2).wait()

    @pl.when(jnp.logical_and(p == pl.num_programs(0) - 1, c == last))
    def _():
        stage_ref[0] = jnp.zeros(stage_ref.shape[1:], F32)

        def zero(s, carry):
            y_copy(s, 0).start()
            y_copy(s, 0).wait()
            return carry

        lax.fori_loop(used_ref[0], y_ref.shape[0] // (tr * slabs), zero, 0)


def _expert_ffn(xs, pass_expert, pass_sub0, pass_nsub, used, w_gu, w_down, layer, tr, cap_sub, tc):
    rows, d = xs.shape
    fe = w_down.shape[2]
    nc = fe // tc
    assert nc >= 2
    slabs = d // LANES
    last = nc - 1

    def chunk(c, pn, p):
        live = jnp.minimum(pn[p], 1)
        return c * live + last * (1 - live)

    grid_spec = pltpu.PrefetchScalarGridSpec(
        num_scalar_prefetch=4,
        grid=(pass_expert.shape[0], nc),
        in_specs=[
            pl.BlockSpec(memory_space=pl.ANY),
            pl.BlockSpec((None, None, d, tc),
                         lambda p, c, pe, ps, pn, us: (layer, pe[p], 0, chunk(c, pn, p))),
            pl.BlockSpec((None, None, d, tc),
                         lambda p, c, pe, ps, pn, us: (layer, pe[p], 0, nc + chunk(c, pn, p))),
            pl.BlockSpec((None, None, tc, d),
                         lambda p, c, pe, ps, pn, us: (layer, pe[p], chunk(c, pn, p), 0)),
        ],
        out_specs=pl.BlockSpec(memory_space=pl.ANY),
        scratch_shapes=[pltpu.VMEM((cap_sub * tr, d), BF16), pltpu.VMEM((cap_sub * tr, d), F32),
                        pltpu.VMEM((d, 2 * tc), BF16), pltpu.VMEM((tc, d), BF16),
                        pltpu.VMEM((2, tr * slabs, LANES), F32), pltpu.SemaphoreType.DMA((3,))],
    )
    return pl.pallas_call(
        functools.partial(_expert_ffn_kernel, tr=tr, slabs=slabs),
        grid_spec=grid_spec,
        out_shape=jax.ShapeDtypeStruct((rows * slabs, LANES), F32),
        compiler_params=_params(("arbitrary", "arbitrary")),
        name="moe_expert_swiglu",
    )(pass_expert, pass_sub0, pass_nsub, used, xs, w_gu, w_gu, w_down)


def _combine_kernel(s1_ref, s2_ref, y_ref, rw_ref, x_ref, g_ref, o_ref, b1_ref, b2_ref, sem,
                    *, slabs):
    i = pl.program_id(0)
    last = pl.num_programs(0) - 1
    tm = x_ref.shape[0]
    slot = i % 2

    def fetch(tile, s, wait):
        _gather_rows(y_ref, b1_ref.at[s], sem.at[0, s], s1_ref, tile * tm, tm, slabs, wait)
        _gather_rows(y_ref, b2_ref.at[s], sem.at[1, s], s2_ref, tile * tm, tm, slabs, wait)

    @pl.when(i == 0)
    def _():
        fetch(0, 0, False)

    @pl.when(i < last)
    def _():
        fetch(i + 1, 1 - slot, False)

    fetch(i, slot, True)

    rw = rw_ref[...]
    w1 = rw[:, 0:1]
    w2 = rw[:, 1:2]
    parts = []
    ssq = jnp.zeros((tm, 1), F32)
    for f in range(slabs):
        m = (w1 * b1_ref[slot, pl.ds(f, tm, stride=_pitch(slabs)), :]
             + w2 * b2_ref[slot, pl.ds(f, tm, stride=_pitch(slabs)), :])
        ssq = ssq + jnp.sum(m * m, axis=1, keepdims=True)
        parts.append(m)
    inv = lax.rsqrt(ssq / (slabs * LANES) + NORM_EPS)
    for f in range(slabs):
        sl = slice(f * LANES, (f + 1) * LANES)
        o_ref[:, sl] = x_ref[:, sl] + parts[f] * inv * g_ref[:, sl]


def _combine(slot1, slot2, y_slabs, rw, x, g, tm):
    n, d = x.shape
    slabs = d // LANES
    grid_spec = pltpu.PrefetchScalarGridSpec(
        num_scalar_prefetch=2,
        grid=(n // tm,),
        in_specs=[
            pl.BlockSpec(memory_space=pl.ANY),
            pl.BlockSpec((tm, LANES), lambda i, s1, s2: (i, 0)),
            pl.BlockSpec((tm, d), lambda i, s1, s2: (i, 0)),
            pl.BlockSpec((1, d), lambda i, s1, s2: (0, 0)),
        ],
        out_specs=pl.BlockSpec((tm, d), lambda i, s1, s2: (i, 0)),
        scratch_shapes=[pltpu.VMEM((2, tm * _pitch(slabs), LANES), F32),
                        pltpu.VMEM((2, tm * _pitch(slabs), LANES), F32),
                        pltpu.SemaphoreType.DMA((2, 2))],
    )
    return pl.pallas_call(
        functools.partial(_combine_kernel, slabs=slabs),
        grid_spec=grid_spec,
        out_shape=jax.ShapeDtypeStruct((n, d), F32),
        compiler_params=_params(("arbitrary",)),
        name="moe_combine",
    )(slot1, slot2, y_slabs, rw, x, g)


def _moe_ffn(x, g_in, g_out, w_router, w_gu, w_down, layer, tiles):
    n, d = x.shape
    experts = w_router.shape[1]
    tr = tiles["expert_rows"]
    cap_sub = tiles["expert_cap"] // tr
    h_slabs, ri, rw, cnt = _router(x, g_in, w_router, tiles["router_rows"])

    counts = cnt[0, :experts]
    nsub = (counts + tr - 1) // tr
    sub_end = jnp.cumsum(nsub)
    sub_start = sub_end - nsub
    slot1 = (sub_start[ri[:, 0]] * tr + ri[:, 2]).astype(jnp.int32)
    slot2 = (sub_start[ri[:, 1]] * tr + ri[:, 3]).astype(jnp.int32)
    rows = TOP_K * n + experts * tr
    n_sub = rows // tr
    token = jnp.arange(n, dtype=jnp.int32)
    tok_of_slot = jnp.zeros((rows,), jnp.int32).at[slot1].set(token).at[slot2].set(token)
    sub_live = (jnp.arange(n_sub, dtype=jnp.int32) < sub_end[-1]).astype(jnp.int32)

    n_pass = experts + n_sub // cap_sub
    passes = (nsub + cap_sub - 1) // cap_sub
    pass_end = jnp.cumsum(passes)
    pidx = jnp.arange(n_pass, dtype=jnp.int32)
    pe = jnp.minimum(jnp.searchsorted(pass_end, pidx, side="right"), experts - 1)
    local = pidx - (pass_end[pe] - passes[pe])
    p_live = pidx < pass_end[-1]
    pass_nsub = jnp.where(p_live, jnp.minimum(cap_sub, nsub[pe] - local * cap_sub), 0)
    pass_sub0 = jnp.where(p_live, sub_start[pe] + local * cap_sub, 0)
    last_pe = jnp.max(jnp.where(passes > 0, jnp.arange(experts), 0))
    pass_expert = jnp.where(p_live, pe, last_pe)

    xs = _dispatch(tok_of_slot, sub_live, h_slabs, d, tr)
    y_slabs = _expert_ffn(xs, pass_expert.astype(jnp.int32), pass_sub0.astype(jnp.int32),
                          pass_nsub.astype(jnp.int32), sub_end[-1:].astype(jnp.int32),
                          w_gu, w_down, layer, tr, cap_sub,
                          tiles["expert_chunk"])
    return _combine(slot1, slot2, y_slabs, rw, x, g_out, tiles["combine_rows"])


def _tiles(n, seq, d, ffn, expert_ffn):
    pick = lambda pref, dim: min(pref, dim)
    return {
        "pw1_rows": pick(512, n), "pw1_cols": pick(1024, d),
        "conv_rows": pick(1024, seq), "conv_cols": pick(256, d),
        "ln_rows": pick(256, n),
        "ffn_rows": pick(512, n), "ffn_chunk": pick(512, ffn),
        "proj_rows": pick(256, seq),
        "attn_q": pick(512, seq), "attn_k": pick(512, seq),
        "out_rows": pick(256, n),
        "router_rows": pick(512, n),
        "expert_rows": pick(256, n), "expert_cap": pick(2304, TOP_K * n),
        "expert_chunk": pick(256, expert_ffn),
        "combine_rows": pick(256, n),
    }


def _rope_tables(seq):
    pos = jnp.arange(seq, dtype=F32)
    inv_freq = ROPE_THETA ** (-jnp.arange(0, QK_ROPE_DIM, 2, dtype=F32) / QK_ROPE_DIM)
    ang = pos[:, None] * inv_freq[None, :]
    cos, sin = jnp.cos(ang), jnp.sin(ang)
    zeros = jnp.zeros((seq, LANES - QK_ROPE_DIM), F32)
    return (jnp.concatenate([cos, cos, zeros], axis=1),
            jnp.concatenate([-sin, sin, zeros], axis=1))


def _swap_halves(w):
    half = w.shape[-1] // 2
    return jnp.concatenate([w[..., half:], w[..., :half]], axis=-1)


def kernel(x, norm_g, conv_w_pw1, conv_b_pw1, conv_w_dw, conv_b_dw, conv_ln_g, conv_ln_b,
           conv_w_pw2, ffn_w_gu, ffn_w_down, moe_w_router, moe_w_gu, moe_w_down,
           kv_in_g, kv_w_a, kv_latent_g, kv_w_b, attn_w_q_a, attn_q_latent_g,
           attn_w_q_b, attn_w_o):
    batch, seq, d = x.shape
    n = batch * seq
    depth = norm_g.shape[0]
    n_conv = depth // 2
    heads = d // 128
    kv_rank = kv_latent_g.shape[0]
    tiles = _tiles(n, seq, d, ffn_w_down.shape[1], moe_w_down.shape[2])
    cos2, sin2 = _rope_tables(seq)
    vec = lambda v: v.reshape(1, -1)

    xs = x.reshape(n, d)
    k = v = None
    for i in range(depth):
        if i == n_conv:
            rope_w = kv_w_a[:, kv_rank:]
            wa_ext = jnp.concatenate([kv_w_a, _swap_halves(rope_w)], axis=1).astype(BF16)
            wb = kv_w_b.reshape(kv_rank, heads, QK_NOPE_DIM + V_DIM)
            wb_split = jnp.concatenate(
                [wb[:, :, :QK_NOPE_DIM].reshape(kv_rank, heads * QK_NOPE_DIM),
                 wb[:, :, QK_NOPE_DIM:].reshape(kv_rank, heads * V_DIM)], axis=1).astype(BF16)
            k, v = _kv_proj(xs, vec(kv_in_g), wa_ext, vec(kv_latent_g), wb_split, cos2, sin2,
                            seq, tiles["proj_rows"])
        if i < n_conv:
            u = _pw1_glu(xs, vec(norm_g[i, 0]), conv_w_pw1[i].astype(BF16), vec(conv_b_pw1[i]),
                         tiles["pw1_rows"], tiles["pw1_cols"])
            c = _dwconv(u, conv_w_dw[i], vec(conv_b_dw[i]), seq, tiles["conv_rows"],
                        tiles["conv_cols"])
            xs = _ln_pw2(c, xs, vec(conv_ln_g[i]), vec(conv_ln_b[i]), conv_w_pw2[i].astype(BF16),
                         vec(norm_g[i, 1]), tiles["ln_rows"])
        else:
            j = i - n_conv
            q_rank = attn_w_q_a.shape[2]
            wqb = attn_w_q_b[j].reshape(q_rank, heads, QK_NOPE_DIM + QK_ROPE_DIM)
            rope_w = wqb[:, :, QK_NOPE_DIM:]
            wqb_ext = jnp.concatenate([wqb, _swap_halves(rope_w)], axis=2)
            wqb_ext = wqb_ext.reshape(q_rank, heads * 256).astype(BF16)
            q = _q_proj(xs, vec(norm_g[i, 0]), attn_w_q_a[j].astype(BF16),
                        vec(attn_q_latent_g[j]), wqb_ext, cos2, sin2, seq, tiles["proj_rows"])
            o = _attention(q, k, v, batch, seq, tiles["attn_q"], tiles["attn_k"])
            xs = _attn_out(o, xs, attn_w_o[j].astype(BF16), vec(norm_g[i, 1]), tiles["out_rows"])
        if i % 2 == 0:
            xs = _dense_ffn(xs, vec(norm_g[i, 2]), vec(norm_g[i, 3]), ffn_w_gu[i // 2].astype(BF16),
                            ffn_w_down[i // 2].astype(BF16), tiles["ffn_rows"], tiles["ffn_chunk"])
        else:
            xs = _moe_ffn(xs, vec(norm_g[i, 2]), vec(norm_g[i, 3]), moe_w_router[i // 2],
                          moe_w_gu, moe_w_down, i // 2, tiles)
    return xs.reshape(batch, seq, d)
```

```python
import functools

import jax
import jax.numpy as jnp
from jax import lax
from jax.experimental import pallas as pl
from jax.experimental.pallas import tpu as pltpu

NORM_EPS = 1e-6
ROPE_THETA = 10000.0
QK_NOPE_DIM = 128
QK_ROPE_DIM = 64
V_DIM = 128
TOP_K = 2

LANES = 128
SUBLANES = 8
CONV_HALO = 32
GATHER_UNROLL = 8
VMEM_LIMIT = 56 * 1024 * 1024

F32 = jnp.float32
BF16 = jnp.bfloat16


def _params(semantics):
    return pltpu.CompilerParams(dimension_semantics=semantics, vmem_limit_bytes=VMEM_LIMIT)


def _rms(x):
    return x * lax.rsqrt(jnp.mean(x * x, axis=-1, keepdims=True) + NORM_EPS)


def _pw1_dwconv_kernel(x_ref, g_ref, w_ref, b_ref, wdw_ref, bdw_ref, o_ref,
                       xn_ref, carry_ref, sh_ref, *, tiles_per_seq, tn, rb, cb):
    tm, d = x_ref.shape
    width = wdw_ref.shape[0]
    ext = tm + CONV_HALO
    base = CONV_HALO - (width - 1)
    first = (pl.program_id(0) % tiles_per_seq) == 0
    xn_ref[...] = (_rms(x_ref[...]) * g_ref[...]).astype(BF16)

    @pl.when(pl.program_id(0) == 0)
    def _():
        carry_ref[...] = jnp.zeros_like(carry_ref)

    for j in range(d // tn):
        cols = slice(j * tn, (j + 1) * tn)
        gate_cols = slice(d + j * tn, d + (j + 1) * tn)
        a = jnp.dot(xn_ref[...], w_ref[:, cols], preferred_element_type=F32) + b_ref[:, cols]
        g = jnp.dot(xn_ref[...], w_ref[:, gate_cols], preferred_element_type=F32) + b_ref[:, gate_cols]
        u = a * jax.nn.sigmoid(g)
        sh = sh_ref.at[j % 2]
        halo = carry_ref[:, cols]
        sh[0, 0:CONV_HALO, :] = jnp.where(first, jnp.zeros_like(halo), halo)
        sh[0, CONV_HALO:ext, :] = u
        carry_ref[:, cols] = u[tm - CONV_HALO:, :]
        for s in range(1, SUBLANES):
            sh[s, 0:ext - SUBLANES, :] = sh[0, s:ext - SUBLANES + s, :]
        for r0 in range(0, tm, rb):
            for c0 in range(0, tn, cb):
                acc = None
                for k in range(width):
                    s = (base + k) % SUBLANES
                    off = (base + k) - s
                    term = (wdw_ref[k:k + 1, j * tn + c0:j * tn + c0 + cb]
                            * sh[s, r0 + off:r0 + off + rb, c0:c0 + cb])
                    acc = term if acc is None else acc + term
                o_ref[r0:r0 + rb, j * tn + c0:j * tn + c0 + cb] = (
                    acc + bdw_ref[:, j * tn + c0:j * tn + c0 + cb]).astype(o_ref.dtype)


def _pw1_dwconv(x, g, w, b, w_dw, b_dw, seq, tm, tn):
    n, d = x.shape
    width = w_dw.shape[0]
    assert width - 1 <= CONV_HALO <= tm
    kern = functools.partial(_pw1_dwconv_kernel, tiles_per_seq=seq // tm, tn=tn,
                             rb=min(64, tm), cb=min(256, tn))
    const = lambda shape: pl.BlockSpec(shape, lambda i: (0, 0), pipeline_mode=pl.Buffered(1))
    return pl.pallas_call(
        kern,
        grid=(n // tm,),
        in_specs=[pl.BlockSpec((tm, d), lambda i: (i, 0)), const((1, d)), const(w.shape),
                  const(b.shape), const((width, d)), const((1, d))],
        out_specs=pl.BlockSpec((tm, d), lambda i: (i, 0)),
        out_shape=jax.ShapeDtypeStruct((n, d), BF16),
        scratch_shapes=[pltpu.VMEM((tm, d), BF16), pltpu.VMEM((CONV_HALO, d), F32),
                        pltpu.VMEM((2, SUBLANES, tm + CONV_HALO, tn), F32)],
        compiler_params=_params(("arbitrary",)),
        name="conv_pw1_dwconv",
    )(x, g, w, b, w_dw, b_dw)


def _ln_pw2_kernel(c_ref, x_ref, lg_ref, lb_ref, w_ref, g_ref, o_ref):
    c = c_ref[...].astype(F32)
    mu = jnp.mean(c, axis=-1, keepdims=True)
    cc = c - mu
    y = cc * lax.rsqrt(jnp.mean(cc * cc, axis=-1, keepdims=True) + NORM_EPS)
    y = y * lg_ref[...] + lb_ref[...]
    y = (y * jax.nn.sigmoid(y)).astype(BF16)
    m = jnp.dot(y, w_ref[...], preferred_element_type=F32)
    o_ref[...] = x_ref[...] + _rms(m) * g_ref[...]


def _ln_pw2(c, x, ln_g, ln_b, w, g, tm):
    n, d = x.shape
    row = pl.BlockSpec((tm, d), lambda i: (i, 0))
    vec = pl.BlockSpec((1, d), lambda i: (0, 0))
    return pl.pallas_call(
        _ln_pw2_kernel,
        grid=(n // tm,),
        in_specs=[row, row, vec, vec, pl.BlockSpec((d, d), lambda i: (0, 0)), vec],
        out_specs=row,
        out_shape=jax.ShapeDtypeStruct((n, d), F32),
        compiler_params=_params(("parallel",)),
        name="conv_ln_pw2",
    )(c, x, ln_g, ln_b, w, g)


def _swiglu_chunk(xn_ref, wg_ref, wu_ref, wd_ref, acc_ref, groups):
    rows = xn_ref.shape[0] // groups
    hids = []
    for i in range(groups):
        xn = xn_ref[i * rows:(i + 1) * rows, :]
        g = jnp.dot(xn, wg_ref[...], preferred_element_type=F32)
        u = jnp.dot(xn, wu_ref[...], preferred_element_type=F32)
        hids.append((g * jax.nn.sigmoid(g) * u).astype(BF16))
    for i in range(groups):
        acc_ref[i * rows:(i + 1) * rows, :] += jnp.dot(hids[i], wd_ref[...],
                                                       preferred_element_type=F32)


def _dense_ffn_kernel(x_ref, gi_ref, go_ref, wg_ref, wu_ref, wd_ref, o_ref, xn_ref, acc_ref):
    c = pl.program_id(1)

    @pl.when(c == 0)
    def _():
        xn_ref[...] = (_rms(x_ref[...]) * gi_ref[...]).astype(BF16)
        acc_ref[...] = jnp.zeros_like(acc_ref)

    _swiglu_chunk(xn_ref, wg_ref, wu_ref, wd_ref, acc_ref, groups=2)

    @pl.when(c == pl.num_programs(1) - 1)
    def _():
        o_ref[...] = x_ref[...] + _rms(acc_ref[...]) * go_ref[...]


def _dense_ffn(x, g_in, g_out, w_gu, w_down, tm, tc):
    n, d = x.shape
    f = w_down.shape[0]
    nc = f // tc
    row = pl.BlockSpec((tm, d), lambda i, c: (i, 0))
    vec = pl.BlockSpec((1, d), lambda i, c: (0, 0))
    return pl.pallas_call(
        _dense_ffn_kernel,
        grid=(n // tm, nc),
        in_specs=[
            row, vec, vec,
            pl.BlockSpec((d, tc), lambda i, c: (0, c)),
            pl.BlockSpec((d, tc), lambda i, c: (0, nc + c)),
            pl.BlockSpec((tc, d), lambda i, c: (c, 0)),
        ],
        out_specs=row,
        out_shape=jax.ShapeDtypeStruct((n, d), F32),
        scratch_shapes=[pltpu.VMEM((tm, d), BF16), pltpu.VMEM((tm, d), F32)],
        compiler_params=_params(("parallel", "arbitrary")),
        name="dense_swiglu",
    )(x, g_in, g_out, w_gu, w_gu, w_down)


def _rope128(t, cos2, sin2):
    return t * cos2 + pltpu.roll(t, 64, axis=1) * sin2


def _kv_proj_kernel(x_ref, g_ref, wa_ref, lg_ref, wb_ref, cos_ref, sin_ref, k_ref, v_ref,
                    *, heads, rank):
    h = (_rms(x_ref[...]) * g_ref[...]).astype(BF16)
    a = jnp.dot(h, wa_ref[...], preferred_element_type=F32)
    c_kv = (_rms(a[:, :rank]) * lg_ref[...]).astype(BF16)
    k_rope = _rope128(a[:, rank:], cos_ref[...], sin_ref[...]).astype(BF16)
    kv = jnp.dot(c_kv, wb_ref[...], preferred_element_type=F32)
    hd = heads * QK_NOPE_DIM
    for hh in range(heads):
        k_ref[:, hh * 256:hh * 256 + 128] = kv[:, hh * 128:(hh + 1) * 128].astype(BF16)
        k_ref[:, hh * 256 + 128:(hh + 1) * 256] = k_rope
    v_ref[...] = kv[:, hd:].astype(BF16)


def _kv_proj(x, g, wa_ext, lat_g, wb_split, cos2, sin2, seq, tm):
    n, d = x.shape
    heads = d // 128
    rank = lat_g.shape[1]
    spt = seq // tm
    kern = functools.partial(_kv_proj_kernel, heads=heads, rank=rank)
    full = lambda a: pl.BlockSpec(a.shape, lambda i: (0, 0))
    return pl.pallas_call(
        kern,
        grid=(n // tm,),
        in_specs=[
            pl.BlockSpec((tm, d), lambda i: (i, 0)), full(g), full(wa_ext), full(lat_g),
            full(wb_split),
            pl.BlockSpec((tm, LANES), lambda i: (i % spt, 0)),
            pl.BlockSpec((tm, LANES), lambda i: (i % spt, 0)),
        ],
        out_specs=[pl.BlockSpec((tm, heads * 256), lambda i: (i, 0)),
                   pl.BlockSpec((tm, heads * 128), lambda i: (i, 0))],
        out_shape=[jax.ShapeDtypeStruct((n, heads * 256), BF16),
                   jax.ShapeDtypeStruct((n, heads * 128), BF16)],
        compiler_params=_params(("parallel",)),
        name="mla_kv_proj",
    )(x, g, wa_ext, lat_g, wb_split, cos2, sin2)


def _q_proj_kernel(x_ref, g_ref, wa_ref, lg_ref, wb_ref, cos_ref, sin_ref, q_ref, *, heads, scale):
    h = (_rms(x_ref[...]) * g_ref[...]).astype(BF16)
    lat = jnp.dot(h, wa_ref[...], preferred_element_type=F32)
    lat = (_rms(lat) * lg_ref[...]).astype(BF16)
    q = jnp.dot(lat, wb_ref[...], preferred_element_type=F32)
    cos2 = cos_ref[...]
    sin2 = sin_ref[...]
    for hh in range(heads):
        q_ref[:, hh * 256:hh * 256 + 128] = (q[:, hh * 256:hh * 256 + 128] * scale).astype(BF16)
        r = _rope128(q[:, hh * 256 + 128:(hh + 1) * 256], cos2, sin2)
        q_ref[:, hh * 256 + 128:(hh + 1) * 256] = (r * scale).astype(BF16)


def _q_proj(x, g, wa, lat_g, wb_ext, cos2, sin2, seq, tm):
    n, d = x.shape
    heads = d // 128
    spt = seq // tm
    scale = float(QK_NOPE_DIM + QK_ROPE_DIM) ** -0.5 * 1.4426950408889634
    kern = functools.partial(_q_proj_kernel, heads=heads, scale=scale)
    full = lambda a: pl.BlockSpec(a.shape, lambda i: (0, 0))
    return pl.pallas_call(
        kern,
        grid=(n // tm,),
        in_specs=[
            pl.BlockSpec((tm, d), lambda i: (i, 0)), full(g), full(wa), full(lat_g), full(wb_ext),
            pl.BlockSpec((tm, LANES), lambda i: (i % spt, 0)),
            pl.BlockSpec((tm, LANES), lambda i: (i % spt, 0)),
        ],
        out_specs=pl.BlockSpec((tm, heads * 256), lambda i: (i, 0)),
        out_shape=jax.ShapeDtypeStruct((n, heads * 256), BF16),
        compiler_params=_params(("parallel",)),
        name="mla_q_proj",
    )(x, g, wa, lat_g, wb_ext, cos2, sin2)


def _attn_kernel(q_ref, k_ref, v_ref, o_ref, *, tq, tk):
    seq = q_ref.shape[0]
    row = lax.broadcasted_iota(jnp.int32, (tq, tk), 0)
    col = lax.broadcasted_iota(jnp.int32, (tq, tk), 1)
    for qi in range(seq // tq):
        q = q_ref[qi * tq:(qi + 1) * tq, :]
        m = jnp.full((tq, 1), -jnp.inf, F32)
        l = jnp.zeros((tq, 1), F32)
        acc = jnp.zeros((tq, V_DIM), F32)
        for j in range((qi + 1) * tq // tk):
            k = k_ref[j * tk:(j + 1) * tk, :]
            s = lax.dot_general(q, k, (((1,), (1,)), ((), ())), preferred_element_type=F32)
            if (j + 1) * tk > qi * tq + 1:
                s = jnp.where(row + qi * tq >= col + j * tk, s, -jnp.inf)
            m_new = jnp.maximum(m, jnp.max(s, axis=1, keepdims=True))
            p = jnp.exp2(s - m_new)
            alpha = jnp.exp2(m - m_new)
            l = alpha * l + jnp.sum(p, axis=1, keepdims=True)
            pv = jnp.dot(p.astype(BF16), v_ref[j * tk:(j + 1) * tk, :],
                         preferred_element_type=F32)
            acc = alpha * acc + pv
            m = m_new
        o_ref[qi * tq:(qi + 1) * tq, :] = (acc / l).astype(o_ref.dtype)


def _attention(q, k, v, batch, seq, tq, tk):
    n = q.shape[0]
    heads = v.shape[1] // V_DIM
    kern = functools.partial(_attn_kernel, tq=tq, tk=tk)
    return pl.pallas_call(
        kern,
        grid=(batch, heads),
        in_specs=[
            pl.BlockSpec((seq, 256), lambda b, h: (b, h)),
            pl.BlockSpec((seq, 256), lambda b, h: (b, h)),
            pl.BlockSpec((seq, V_DIM), lambda b, h: (b, h)),
        ],
        out_specs=pl.BlockSpec((seq, V_DIM), lambda b, h: (b, h)),
        out_shape=jax.ShapeDtypeStruct((n, heads * V_DIM), BF16),
        compiler_params=_params(("parallel", "parallel")),
        name="mla_attention",
    )(q, k, v)


def _attn_out_kernel(o_ref, x_ref, w_ref, g_ref, xo_ref):
    m = jnp.dot(o_ref[...], w_ref[...], preferred_element_type=F32)
    xo_ref[...] = x_ref[...] + _rms(m) * g_ref[...]


def _attn_out(o, x, w, g, tm):
    n, d = x.shape
    row = pl.BlockSpec((tm, d), lambda i: (i, 0))
    return pl.pallas_call(
        _attn_out_kernel,
        grid=(n // tm,),
        in_specs=[pl.BlockSpec((tm, o.shape[1]), lambda i: (i, 0)), row,
                  pl.BlockSpec(w.shape, lambda i: (0, 0)), pl.BlockSpec((1, d), lambda i: (0, 0))],
        out_specs=row,
        out_shape=jax.ShapeDtypeStruct((n, d), F32),
        compiler_params=_params(("parallel",)),
        name="mla_out_proj",
    )(o, x, w, g)


def _router_kernel(x_ref, g_ref, wcat_ref, h_ref, ri_ref, rw_ref, cnt_ref, base_ref,
                   *, experts, slabs):
    tm = x_ref.shape[0]

    @pl.when(pl.program_id(0) == 0)
    def _():
        base_ref[...] = jnp.zeros_like(base_ref)

    h = _rms(x_ref[...]) * g_ref[...]
    for f in range(slabs):
        h_ref[pl.ds(f, tm, stride=slabs), :] = h[:, f * LANES:(f + 1) * LANES]

    h_hi = h.astype(BF16)
    h_lo = (h - h_hi.astype(F32)).astype(BF16)
    hi = jnp.dot(h_hi, wcat_ref[...], preferred_element_type=F32)
    logits = (hi[:, :LANES] + hi[:, LANES:]
              + jnp.dot(h_lo, wcat_ref[:, :LANES], preferred_element_type=F32))

    lane = lax.broadcasted_iota(jnp.int32, (tm, LANES), 1)
    lg = jnp.where(lane < experts, logits, -jnp.inf)
    m1 = jnp.max(lg, axis=1, keepdims=True)
    i1 = jnp.min(jnp.where(lg == m1, lane, LANES), axis=1, keepdims=True)
    oh1 = lane == i1
    lg2 = jnp.where(oh1, -jnp.inf, lg)
    m2 = jnp.max(lg2, axis=1, keepdims=True)
    i2 = jnp.min(jnp.where(lg2 == m2, lane, LANES), axis=1, keepdims=True)
    oh2 = lane == i2
    e2 = jnp.exp(m2 - m1)
    w1 = 1.0 / (1.0 + e2)
    w2 = e2 / (1.0 + e2)

    cnt = oh1.astype(F32) + oh2.astype(F32)
    r_i = lax.broadcasted_iota(jnp.int32, (tm, tm), 0)
    c_i = lax.broadcasted_iota(jnp.int32, (tm, tm), 1)
    tri = (c_i < r_i).astype(BF16)
    before = jnp.dot(tri, cnt.astype(BF16), preferred_element_type=F32) + base_ref[0:1, :]
    rank1 = jnp.sum(jnp.where(oh1, before, 0.0), axis=1, keepdims=True).astype(jnp.int32)
    rank2 = jnp.sum(jnp.where(oh2, before, 0.0), axis=1, keepdims=True).astype(jnp.int32)
    total = base_ref[0:1, :] + jnp.sum(cnt, axis=0, keepdims=True)
    base_ref[...] = jnp.broadcast_to(total, base_ref.shape)
    cnt_ref[...] = jnp.broadcast_to(total, cnt_ref.shape).astype(jnp.int32)

    ri_ref[...] = jnp.where(lane == 0, i1, jnp.where(lane == 1, i2,
                            jnp.where(lane == 2, rank1, jnp.where(lane == 3, rank2, 0))))
    rw_ref[...] = jnp.where(lane == 0, w1, jnp.where(lane == 1, w2, 0.0))


def _router(x, g, w_router, tm):
    n, d = x.shape
    experts = w_router.shape[1]
    slabs = d // LANES
    w_pad = jnp.zeros((d, LANES), F32).at[:, :experts].set(w_router)
    w_hi = w_pad.astype(BF16)
    w_lo = (w_pad - w_hi.astype(F32)).astype(BF16)
    w_cat = jnp.concatenate([w_hi, w_lo], axis=1)
    kern = functools.partial(_router_kernel, experts=experts, slabs=slabs)
    return pl.pallas_call(
        kern,
        grid=(n // tm,),
        in_specs=[pl.BlockSpec((tm, d), lambda i: (i, 0)), pl.BlockSpec((1, d), lambda i: (0, 0)),
                  pl.BlockSpec((d, 2 * LANES), lambda i: (0, 0))],
        out_specs=[pl.BlockSpec((tm * slabs, LANES), lambda i: (i, 0)),
                   pl.BlockSpec((tm, LANES), lambda i: (i, 0)),
                   pl.BlockSpec((tm, LANES), lambda i: (i, 0)),
                   pl.BlockSpec((8, LANES), lambda i: (0, 0))],
        out_shape=[jax.ShapeDtypeStruct((n * slabs, LANES), F32),
                   jax.ShapeDtypeStruct((n, LANES), jnp.int32),
                   jax.ShapeDtypeStruct((n, LANES), F32),
                   jax.ShapeDtypeStruct((8, LANES), jnp.int32)],
        scratch_shapes=[pltpu.VMEM((8, LANES), F32)],
        compiler_params=_params(("arbitrary",)),
        name="moe_router",
    )(x, g, w_cat)


def _pitch(slabs):
    groups = slabs // SUBLANES
    return slabs if groups % 2 == 1 else slabs + SUBLANES


def _row_copy(src_ref, dst_ref, sem, src_row, dst_row, slabs):
    pitch = _pitch(slabs)
    return pltpu.make_async_copy(
        src_ref.at[pl.ds(pl.multiple_of(src_row * slabs, slabs), slabs), :],
        dst_ref.at[pl.ds(pl.multiple_of(dst_row * pitch, SUBLANES), slabs), :],
        sem)


def _gather_rows(src_ref, dst_ref, sem, idx_ref, base, count, slabs, wait):
    def body(q, carry):
        for j in range(GATHER_UNROLL):
            r = q * GATHER_UNROLL + j
            cp = _row_copy(src_ref, dst_ref, sem, idx_ref[base + r], r, slabs)
            if wait:
                cp.wait()
            else:
                cp.start(priority=j % 2)
        return carry

    assert count % GATHER_UNROLL == 0
    lax.fori_loop(0, count // GATHER_UNROLL, body, 0)


def _dispatch_kernel(tok_ref, live_ref, h_ref, o_ref, gbuf_ref, sem, *, slabs):
    i = pl.program_id(0)
    last = pl.num_programs(0) - 1
    tr = o_ref.shape[0]
    slot = i % 2
    nxt = jnp.minimum(i + 1, last)

    @pl.when(jnp.logical_and(i == 0, live_ref[0] == 1))
    def _():
        _gather_rows(h_ref, gbuf_ref.at[0], sem.at[0], tok_ref, 0, tr, slabs, wait=False)

    @pl.when(jnp.logical_and(i < last, live_ref[nxt] == 1))
    def _():
        _gather_rows(h_ref, gbuf_ref.at[1 - slot], sem.at[1 - slot], tok_ref, nxt * tr, tr, slabs,
                     wait=False)

    @pl.when(live_ref[i] == 1)
    def _():
        _gather_rows(h_ref, gbuf_ref.at[slot], sem.at[slot], tok_ref, i * tr, tr, slabs, wait=True)
        for f in range(slabs):
            o_ref[:, f * LANES:(f + 1) * LANES] = (
                gbuf_ref[slot, pl.ds(f, tr, stride=_pitch(slabs)), :].astype(BF16))

    @pl.when(live_ref[i] == 0)
    def _():
        o_ref[...] = jnp.zeros_like(o_ref)


def _dispatch(tok_of_slot, sub_live, h_slabs, d, tr):
    rows = tok_of_slot.shape[0]
    slabs = d // LANES
    grid_spec = pltpu.PrefetchScalarGridSpec(
        num_scalar_prefetch=2,
        grid=(rows // tr,),
        in_specs=[pl.BlockSpec(memory_space=pl.ANY)],
        out_specs=pl.BlockSpec((tr, d), lambda i, tok, live: (i, 0)),
        scratch_shapes=[pltpu.VMEM((2, tr * _pitch(slabs), LANES), F32),
                        pltpu.SemaphoreType.DMA((2,))],
    )
    return pl.pallas_call(
        functools.partial(_dispatch_kernel, slabs=slabs),
        grid_spec=grid_spec,
        out_shape=jax.ShapeDtypeStruct((rows, d), BF16),
        compiler_params=_params(("arbitrary",)),
        name="moe_dispatch",
    )(tok_of_slot, sub_live, h_slabs)


def _expert_ffn_kernel(pe_ref, ps_ref, pn_ref, used_ref, xs_ref, wg_ref, wu_ref, wd_ref, y_ref,
                       xv_ref, acc_ref, stage_ref, sem, *, tr, slabs):
    p = pl.program_id(0)
    c = pl.program_id(1)
    last = pl.num_programs(1) - 1
    nsub = pn_ref[p]
    sub0 = ps_ref[p]
    live = nsub > 0

    def rows_of(r):
        return pl.ds(pl.multiple_of(r * tr, tr), tr)

    def x_copy(r):
        return pltpu.make_async_copy(
            xs_ref.at[pl.ds(pl.multiple_of((sub0 + r) * tr, tr), tr), :],
            xv_ref.at[rows_of(r), :], sem.at[0])

    def x_start(r, carry):
        x_copy(r).start()
        return carry

    def x_wait(r, carry):
        x_copy(r).wait()
        return carry

    @pl.when(jnp.logical_and(live, c == 0))
    def _():
        lax.fori_loop(0, nsub, x_start, 0)
        lax.fori_loop(0, nsub, x_wait, 0)

    def hidden(r):
        x = xv_ref[rows_of(r), :]
        g = jnp.dot(x, wg_ref[...].astype(BF16), preferred_element_type=F32)
        u = jnp.dot(x, wu_ref[...].astype(BF16), preferred_element_type=F32)
        return (g * jax.nn.sigmoid(g) * u).astype(BF16)

    def y_copy(sub, slot):
        return pltpu.make_async_copy(
            stage_ref.at[slot],
            y_ref.at[pl.ds(pl.multiple_of(sub * tr * slabs, tr * slabs), tr * slabs), :],
            sem.at[1 + slot])

    def first_apply(r, slot, part):
        acc_ref[rows_of(r), :] = part

    def middle_apply(r, slot, part):
        acc_ref[rows_of(r), :] += part

    def final_apply(r, slot, part):
        res = acc_ref[rows_of(r), :] + part

        @pl.when(r >= 2)
        def _():
            y_copy(sub0 + r - 2, slot).wait()

        for f in range(slabs):
            stage_ref[slot, pl.ds(f, tr, stride=slabs), :] = res[:, f * LANES:(f + 1) * LANES]
        y_copy(sub0 + r, slot).start()

    def for_each_subtile(apply, group):
        def run(first, count):
            hids = [hidden(first + t) for t in range(count)]
            for t in range(count):
                part = jnp.dot(hids[t], wd_ref[...].astype(BF16), preferred_element_type=F32)
                apply(first + t, t % 2, part)

        def body(q, carry):
            run(group * q, group)
            return carry

        lax.fori_loop(0, nsub // group, body, 0)
        rest = nsub % group
        size = group // 2
        while size >= 1:
            @pl.when((rest // size) % 2 == 1)
            def _(size=size):
                run(nsub - rest % (2 * size), size)
            size //= 2

    @pl.when(live)
    def _():
        @pl.when(c == 0)
        def _():
            for_each_subtile(first_apply, 2)

        @pl.when(jnp.logical_and(c > 0, c < last))
        def _():
            for_each_subtile(middle_apply, 4)

        @pl.when(c == last)
        def _():
            for_each_subtile(final_apply, 2)

            @pl.when(nsub >= 2)
            def _():
                y_copy(sub0 + nsub - 2, nsub % 2).wait()

            y_copy(sub0 + nsub - 1, (nsub - 1) % 2).wait()

    @pl.when(jnp.logical_and(p == pl.num_programs(0) - 1, c == last))
    def _():
        stage_ref[0] = jnp.zeros(stage_ref.shape[1:], F32)

        def zero(s, carry):
            y_copy(s, 0).start()
            y_copy(s, 0).wait()
            return carry

        lax.fori_loop(used_ref[0], y_ref.shape[0] // (tr * slabs), zero, 0)


def _expert_ffn(xs, pass_expert, pass_sub0, pass_nsub, used, w_gu, w_down, layer, tr, cap_sub, tc):
    rows, d = xs.shape
    fe = w_down.shape[2]
    nc = fe // tc
    assert nc >= 2
    slabs = d // LANES
    last = nc - 1

    def chunk(c, pn, p):
        live = jnp.minimum(pn[p], 1)
        return c * live + last * (1 - live)

    grid_spec = pltpu.PrefetchScalarGridSpec(
        num_scalar_prefetch=4,
        grid=(pass_expert.shape[0], nc),
        in_specs=[
            pl.BlockSpec(memory_space=pl.ANY),
            pl.BlockSpec((None, None, d, tc),
                         lambda p, c, pe, ps, pn, us: (layer, pe[p], 0, chunk(c, pn, p))),
            pl.BlockSpec((None, None, d, tc),
                         lambda p, c, pe, ps, pn, us: (layer, pe[p], 0, nc + chunk(c, pn, p))),
            pl.BlockSpec((None, None, tc, d),
                         lambda p, c, pe, ps, pn, us: (layer, pe[p], chunk(c, pn, p), 0)),
        ],
        out_specs=pl.BlockSpec(memory_space=pl.ANY),
        scratch_shapes=[pltpu.VMEM((cap_sub * tr, d), BF16), pltpu.VMEM((cap_sub * tr, d), F32),
                        pltpu.VMEM((2, tr * slabs, LANES), F32), pltpu.SemaphoreType.DMA((3,))],
    )
    return pl.pallas_call(
        functools.partial(_expert_ffn_kernel, tr=tr, slabs=slabs),
        grid_spec=grid_spec,
        out_shape=jax.ShapeDtypeStruct((rows * slabs, LANES), F32),
        compiler_params=_params(("arbitrary", "arbitrary")),
        name="moe_expert_swiglu",
    )(pass_expert, pass_sub0, pass_nsub, used, xs, w_gu, w_gu, w_down)


def _combine_kernel(s1_ref, s2_ref, y_ref, rw_ref, x_ref, g_ref, o_ref, b1_ref, b2_ref, sem,
                    *, slabs):
    i = pl.program_id(0)
    last = pl.num_programs(0) - 1
    tm = x_ref.shape[0]
    slot = i % 2

    def fetch(tile, s, wait):
        _gather_rows(y_ref, b1_ref.at[s], sem.at[0, s], s1_ref, tile * tm, tm, slabs, wait)
        _gather_rows(y_ref, b2_ref.at[s], sem.at[1, s], s2_ref, tile * tm, tm, slabs, wait)

    @pl.when(i == 0)
    def _():
        fetch(0, 0, False)

    @pl.when(i < last)
    def _():
        fetch(i + 1, 1 - slot, False)

    fetch(i, slot, True)

    rw = rw_ref[...]
    w1 = rw[:, 0:1]
    w2 = rw[:, 1:2]
    parts = []
    ssq = jnp.zeros((tm, 1), F32)
    for f in range(slabs):
        m = (w1 * b1_ref[slot, pl.ds(f, tm, stride=_pitch(slabs)), :]
             + w2 * b2_ref[slot, pl.ds(f, tm, stride=_pitch(slabs)), :])
        ssq = ssq + jnp.sum(m * m, axis=1, keepdims=True)
        parts.append(m)
    inv = lax.rsqrt(ssq / (slabs * LANES) + NORM_EPS)
    for f in range(slabs):
        sl = slice(f * LANES, (f + 1) * LANES)
        o_ref[:, sl] = x_ref[:, sl] + parts[f] * inv * g_ref[:, sl]


def _combine(slot1, slot2, y_slabs, rw, x, g, tm):
    n, d = x.shape
    slabs = d // LANES
    grid_spec = pltpu.PrefetchScalarGridSpec(
        num_scalar_prefetch=2,
        grid=(n // tm,),
        in_specs=[
            pl.BlockSpec(memory_space=pl.ANY),
            pl.BlockSpec((tm, LANES), lambda i, s1, s2: (i, 0)),
            pl.BlockSpec((tm, d), lambda i, s1, s2: (i, 0)),
            pl.BlockSpec((1, d), lambda i, s1, s2: (0, 0)),
        ],
        out_specs=pl.BlockSpec((tm, d), lambda i, s1, s2: (i, 0)),
        scratch_shapes=[pltpu.VMEM((2, tm * _pitch(slabs), LANES), F32),
                        pltpu.VMEM((2, tm * _pitch(slabs), LANES), F32),
                        pltpu.SemaphoreType.DMA((2, 2))],
    )
    return pl.pallas_call(
        functools.partial(_combine_kernel, slabs=slabs),
        grid_spec=grid_spec,
        out_shape=jax.ShapeDtypeStruct((n, d), F32),
        compiler_params=_params(("arbitrary",)),
        name="moe_combine",
    )(slot1, slot2, y_slabs, rw, x, g)


def _moe_ffn(x, g_in, g_out, w_router, w_gu, w_down, layer, tiles):
    n, d = x.shape
    experts = w_router.shape[1]
    tr = tiles["expert_rows"]
    cap_sub = tiles["expert_cap"] // tr
    h_slabs, ri, rw, cnt = _router(x, g_in, w_router, tiles["router_rows"])

    counts = cnt[0, :experts]
    nsub = (counts + tr - 1) // tr
    sub_end = jnp.cumsum(nsub)
    sub_start = sub_end - nsub
    slot1 = (sub_start[ri[:, 0]] * tr + ri[:, 2]).astype(jnp.int32)
    slot2 = (sub_start[ri[:, 1]] * tr + ri[:, 3]).astype(jnp.int32)
    rows = TOP_K * n + experts * tr
    n_sub = rows // tr
    token = jnp.arange(n, dtype=jnp.int32)
    tok_of_slot = jnp.zeros((rows,), jnp.int32).at[jnp.concatenate([slot1, slot2])].set(
        jnp.concatenate([token, token]), unique_indices=True)
    sub_live = (jnp.arange(n_sub, dtype=jnp.int32) < sub_end[-1]).astype(jnp.int32)

    n_pass = experts + n_sub // cap_sub
    passes = (nsub + cap_sub - 1) // cap_sub
    pass_end = jnp.cumsum(passes)
    pidx = jnp.arange(n_pass, dtype=jnp.int32)
    pe = jnp.minimum(jnp.searchsorted(pass_end, pidx, side="right"), experts - 1)
    local = pidx - (pass_end[pe] - passes[pe])
    p_live = pidx < pass_end[-1]
    pass_nsub = jnp.where(p_live, jnp.minimum(cap_sub, nsub[pe] - local * cap_sub), 0)
    pass_sub0 = jnp.where(p_live, sub_start[pe] + local * cap_sub, 0)
    last_pe = jnp.max(jnp.where(passes > 0, jnp.arange(experts), 0))
    pass_expert = jnp.where(p_live, pe, last_pe)

    xs = _dispatch(tok_of_slot, sub_live, h_slabs, d, tr)
    y_slabs = _expert_ffn(xs, pass_expert.astype(jnp.int32), pass_sub0.astype(jnp.int32),
                          pass_nsub.astype(jnp.int32), sub_end[-1:].astype(jnp.int32),
                          w_gu, w_down, layer, tr, cap_sub,
                          tiles["expert_chunk"])
    return _combine(slot1, slot2, y_slabs, rw, x, g_out, tiles["combine_rows"])


def _tiles(n, seq, d, ffn, expert_ffn):
    pick = lambda pref, dim: min(pref, dim)
    return {
        "conv_rows": pick(256, seq), "conv_cols": pick(512, d),
        "ln_rows": pick(256, n),
        "ffn_rows": pick(512, n), "ffn_chunk": pick(512, ffn),
        "proj_rows": pick(256, seq),
        "attn_q": pick(512, seq), "attn_k": pick(512, seq),
        "out_rows": pick(256, n),
        "router_rows": pick(512, n),
        "expert_rows": pick(256, n), "expert_cap": pick(2304, TOP_K * n),
        "expert_chunk": pick(256, expert_ffn),
        "combine_rows": pick(256, n),
    }


def _rope_tables(seq):
    pos = jnp.arange(seq, dtype=F32)
    inv_freq = ROPE_THETA ** (-jnp.arange(0, QK_ROPE_DIM, 2, dtype=F32) / QK_ROPE_DIM)
    ang = pos[:, None] * inv_freq[None, :]
    cos, sin = jnp.cos(ang), jnp.sin(ang)
    zeros = jnp.zeros((seq, LANES - QK_ROPE_DIM), F32)
    return (jnp.concatenate([cos, cos, zeros], axis=1),
            jnp.concatenate([-sin, sin, zeros], axis=1))


def _swap_halves(w):
    half = w.shape[-1] // 2
    return jnp.concatenate([w[..., half:], w[..., :half]], axis=-1)


def kernel(x, norm_g, conv_w_pw1, conv_b_pw1, conv_w_dw, conv_b_dw, conv_ln_g, conv_ln_b,
           conv_w_pw2, ffn_w_gu, ffn_w_down, moe_w_router, moe_w_gu, moe_w_down,
           kv_in_g, kv_w_a, kv_latent_g, kv_w_b, attn_w_q_a, attn_q_latent_g,
           attn_w_q_b, attn_w_o):
    batch, seq, d = x.shape
    n = batch * seq
    depth = norm_g.shape[0]
    n_conv = depth // 2
    heads = d // 128
    kv_rank = kv_latent_g.shape[0]
    tiles = _tiles(n, seq, d, ffn_w_down.shape[1], moe_w_down.shape[2])
    cos2, sin2 = _rope_tables(seq)
    vec = lambda v: v.reshape(1, -1)

    xs = x.reshape(n, d)
    k = v = None
    for i in range(depth):
        if i == n_conv:
            rope_w = kv_w_a[:, kv_rank:]
            wa_ext = jnp.concatenate([kv_w_a, _swap_halves(rope_w)], axis=1).astype(BF16)
            wb = kv_w_b.reshape(kv_rank, heads, QK_NOPE_DIM + V_DIM)
            wb_split = jnp.concatenate(
                [wb[:, :, :QK_NOPE_DIM].reshape(kv_rank, heads * QK_NOPE_DIM),
                 wb[:, :, QK_NOPE_DIM:].reshape(kv_rank, heads * V_DIM)], axis=1).astype(BF16)
            k, v = _kv_proj(xs, vec(kv_in_g), wa_ext, vec(kv_latent_g), wb_split, cos2, sin2,
                            seq, tiles["proj_rows"])
        if i < n_conv:
            c = _pw1_dwconv(xs, vec(norm_g[i, 0]), conv_w_pw1[i].astype(BF16), vec(conv_b_pw1[i]),
                            conv_w_dw[i], vec(conv_b_dw[i]), seq, tiles["conv_rows"],
                            tiles["conv_cols"])
            xs = _ln_pw2(c, xs, vec(conv_ln_g[i]), vec(conv_ln_b[i]), conv_w_pw2[i].astype(BF16),
                         vec(norm_g[i, 1]), tiles["ln_rows"])
        else:
            j = i - n_conv
            q_rank = attn_w_q_a.shape[2]
            wqb = attn_w_q_b[j].reshape(q_rank, heads, QK_NOPE_DIM + QK_ROPE_DIM)
            rope_w = wqb[:, :, QK_NOPE_DIM:]
            wqb_ext = jnp.concatenate([wqb, _swap_halves(rope_w)], axis=2)
            wqb_ext = wqb_ext.reshape(q_rank, heads * 256).astype(BF16)
            q = _q_proj(xs, vec(norm_g[i, 0]), attn_w_q_a[j].astype(BF16),
                        vec(attn_q_latent_g[j]), wqb_ext, cos2, sin2, seq, tiles["proj_rows"])
            o = _attention(q, k, v, batch, seq, tiles["attn_q"], tiles["attn_k"])
            xs = _attn_out(o, xs, attn_w_o[j].astype(BF16), vec(norm_g[i, 1]), tiles["out_rows"])
        if i % 2 == 0:
            xs = _dense_ffn(xs, vec(norm_g[i, 2]), vec(norm_g[i, 3]), ffn_w_gu[i // 2].astype(BF16),
                            ffn_w_down[i // 2].astype(BF16), tiles["ffn_rows"], tiles["ffn_chunk"])
        else:
            xs = _moe_ffn(xs, vec(norm_g[i, 2]), vec(norm_g[i, 3]), moe_w_router[i // 2],
                          moe_w_gu, moe_w_down, i // 2, tiles)
    return xs.reshape(batch, seq, d)
```

```python
import functools

import jax
import jax.numpy as jnp
from jax import lax
from jax.experimental import pallas as pl
from jax.experimental.pallas import tpu as pltpu

NORM_EPS = 1e-6
ROPE_THETA = 10000.0
QK_NOPE_DIM = 128
QK_ROPE_DIM = 64
V_DIM = 128
TOP_K = 2
_Q_SCALE = float(QK_NOPE_DIM + QK_ROPE_DIM) ** -0.5 * 1.4426950408889634

LANES = 128
SUBLANES = 8
CONV_HALO = 32
GATHER_UNROLL = 8
GATHER_DEPTH = 3
VMEM_LIMIT = 56 * 1024 * 1024

F32 = jnp.float32
BF16 = jnp.bfloat16


def _params(semantics):
    return pltpu.CompilerParams(dimension_semantics=semantics, vmem_limit_bytes=VMEM_LIMIT)


def _rms(x):
    return x * lax.rsqrt(jnp.mean(x * x, axis=-1, keepdims=True) + NORM_EPS)


def _pw1_dwconv_kernel(x_ref, g_ref, w_ref, b_ref, wdw_ref, bdw_ref, o_ref,
                       xn_ref, carry_ref, sh_ref, *, tiles_per_seq, tn, rb, cb):
    tm, d = x_ref.shape
    width = wdw_ref.shape[0]
    ext = tm + CONV_HALO
    base = CONV_HALO - (width - 1)
    first = (pl.program_id(0) % tiles_per_seq) == 0
    xn_ref[...] = (_rms(x_ref[...]) * g_ref[...]).astype(BF16)

    @pl.when(pl.program_id(0) == 0)
    def _():
        carry_ref[...] = jnp.zeros_like(carry_ref)

    for j in range(d // tn):
        cols = slice(j * tn, (j + 1) * tn)
        gate_cols = slice(d + j * tn, d + (j + 1) * tn)
        a = jnp.dot(xn_ref[...], w_ref[:, cols], preferred_element_type=F32) + b_ref[:, cols]
        g = jnp.dot(xn_ref[...], w_ref[:, gate_cols], preferred_element_type=F32) + b_ref[:, gate_cols]
        u = a * jax.nn.sigmoid(g)
        sh = sh_ref.at[j % 2]
        halo = carry_ref[:, cols]
        sh[0, 0:CONV_HALO, :] = jnp.where(first, jnp.zeros_like(halo), halo)
        sh[0, CONV_HALO:ext, :] = u
        carry_ref[:, cols] = u[tm - CONV_HALO:, :]
        for s in range(1, SUBLANES):
            sh[s, 0:ext - SUBLANES, :] = sh[0, s:ext - SUBLANES + s, :]
        for r0 in range(0, tm, rb):
            for c0 in range(0, tn, cb):
                acc = None
                for k in range(width):
                    s = (base + k) % SUBLANES
                    off = (base + k) - s
                    term = (wdw_ref[k:k + 1, j * tn + c0:j * tn + c0 + cb]
                            * sh[s, r0 + off:r0 + off + rb, c0:c0 + cb])
                    acc = term if acc is None else acc + term
                o_ref[r0:r0 + rb, j * tn + c0:j * tn + c0 + cb] = (
                    acc + bdw_ref[:, j * tn + c0:j * tn + c0 + cb]).astype(o_ref.dtype)


def _pw1_dwconv(x, g, w, b, w_dw, b_dw, seq, tm, tn):
    n, d = x.shape
    width = w_dw.shape[0]
    assert width - 1 <= CONV_HALO <= tm
    kern = functools.partial(_pw1_dwconv_kernel, tiles_per_seq=seq // tm, tn=tn,
                             rb=min(64, tm), cb=min(256, tn))
    const = lambda shape: pl.BlockSpec(shape, lambda i: (0, 0), pipeline_mode=pl.Buffered(1))
    return pl.pallas_call(
        kern,
        grid=(n // tm,),
        in_specs=[pl.BlockSpec((tm, d), lambda i: (i, 0)), const((1, d)), const(w.shape),
                  const(b.shape), const((width, d)), const((1, d))],
        out_specs=pl.BlockSpec((tm, d), lambda i: (i, 0)),
        out_shape=jax.ShapeDtypeStruct((n, d), BF16),
        scratch_shapes=[pltpu.VMEM((tm, d), BF16), pltpu.VMEM((CONV_HALO, d), F32),
                        pltpu.VMEM((2, SUBLANES, tm + CONV_HALO, tn), F32)],
        compiler_params=_params(("arbitrary",)),
        name="conv_pw1_dwconv",
    )(x, g, w, b, w_dw, b_dw)


def _ln_pw2_kernel(c_ref, x_ref, lg_ref, lb_ref, w_ref, g_ref, o_ref):
    c = c_ref[...].astype(F32)
    mu = jnp.mean(c, axis=-1, keepdims=True)
    cc = c - mu
    y = cc * lax.rsqrt(jnp.mean(cc * cc, axis=-1, keepdims=True) + NORM_EPS)
    y = y * lg_ref[...] + lb_ref[...]
    y = (y * jax.nn.sigmoid(y)).astype(BF16)
    m = jnp.dot(y, w_ref[...], preferred_element_type=F32)
    o_ref[...] = x_ref[...] + _rms(m) * g_ref[...]


def _ln_pw2(c, x, ln_g, ln_b, w, g, tm):
    n, d = x.shape
    row = pl.BlockSpec((tm, d), lambda i: (i, 0))
    vec = pl.BlockSpec((1, d), lambda i: (0, 0))
    return pl.pallas_call(
        _ln_pw2_kernel,
        grid=(n // tm,),
        in_specs=[row, row, vec, vec, pl.BlockSpec((d, d), lambda i: (0, 0)), vec],
        out_specs=row,
        out_shape=jax.ShapeDtypeStruct((n, d), F32),
        compiler_params=_params(("parallel",)),
        name="conv_ln_pw2",
    )(c, x, ln_g, ln_b, w, g)


def _swiglu_chunk(xn_ref, wg_ref, wu_ref, wd_ref, acc_ref, groups):
    rows = xn_ref.shape[0] // groups
    hids = []
    for i in range(groups):
        xn = xn_ref[i * rows:(i + 1) * rows, :]
        g = jnp.dot(xn, wg_ref[...], preferred_element_type=F32)
        u = jnp.dot(xn, wu_ref[...], preferred_element_type=F32)
        hids.append((g * jax.nn.sigmoid(g) * u).astype(BF16))
    for i in range(groups):
        acc_ref[i * rows:(i + 1) * rows, :] += jnp.dot(hids[i], wd_ref[...],
                                                       preferred_element_type=F32)


def _dense_ffn_kernel(x_ref, gi_ref, go_ref, wg_ref, wu_ref, wd_ref, o_ref, xn_ref, acc_ref):
    c = pl.program_id(1)

    @pl.when(c == 0)
    def _():
        xn_ref[...] = (_rms(x_ref[...]) * gi_ref[...]).astype(BF16)
        acc_ref[...] = jnp.zeros_like(acc_ref)

    _swiglu_chunk(xn_ref, wg_ref, wu_ref, wd_ref, acc_ref, groups=2)

    @pl.when(c == pl.num_programs(1) - 1)
    def _():
        o_ref[...] = x_ref[...] + _rms(acc_ref[...]) * go_ref[...]


def _dense_ffn(x, g_in, g_out, w_gu, w_down, tm, tc):
    n, d = x.shape
    f = w_down.shape[0]
    nc = f // tc
    row = pl.BlockSpec((tm, d), lambda i, c: (i, 0))
    vec = pl.BlockSpec((1, d), lambda i, c: (0, 0))
    return pl.pallas_call(
        _dense_ffn_kernel,
        grid=(n // tm, nc),
        in_specs=[
            row, vec, vec,
            pl.BlockSpec((d, tc), lambda i, c: (0, c)),
            pl.BlockSpec((d, tc), lambda i, c: (0, nc + c)),
            pl.BlockSpec((tc, d), lambda i, c: (c, 0)),
        ],
        out_specs=row,
        out_shape=jax.ShapeDtypeStruct((n, d), F32),
        scratch_shapes=[pltpu.VMEM((tm, d), BF16), pltpu.VMEM((tm, d), F32)],
        compiler_params=_params(("parallel", "arbitrary")),
        name="dense_swiglu",
    )(x, g_in, g_out, w_gu, w_gu, w_down)


def _rope128(t, cos2, sin2):
    return t * cos2 + pltpu.roll(t, 64, axis=1) * sin2


def _q_proj_kernel(x_ref, g_ref, wa_ref, lg_ref, wb_ref, cos_ref, sin_ref, q_ref, *, heads, scale):
    h = (_rms(x_ref[...]) * g_ref[...]).astype(BF16)
    lat = jnp.dot(h, wa_ref[...], preferred_element_type=F32)
    lat = (_rms(lat) * lg_ref[...]).astype(BF16)
    q = jnp.dot(lat, wb_ref[...], preferred_element_type=F32)
    cos2 = cos_ref[...]
    sin2 = sin_ref[...]
    for hh in range(heads):
        q_ref[:, hh * 256:hh * 256 + 128] = (q[:, hh * 256:hh * 256 + 128] * scale).astype(BF16)
        r = _rope128(q[:, hh * 256 + 128:(hh + 1) * 256], cos2, sin2)
        q_ref[:, hh * 256 + 128:(hh + 1) * 256] = (r * scale).astype(BF16)


def _q_proj(x, g, wa, lat_g, wb_ext, cos2, sin2, seq, tm):
    n, d = x.shape
    heads = d // 128
    spt = seq // tm
    kern = functools.partial(_q_proj_kernel, heads=heads, scale=_Q_SCALE)
    full = lambda a: pl.BlockSpec(a.shape, lambda i: (0, 0))
    return pl.pallas_call(
        kern,
        grid=(n // tm,),
        in_specs=[
            pl.BlockSpec((tm, d), lambda i: (i, 0)), full(g), full(wa), full(lat_g), full(wb_ext),
            pl.BlockSpec((tm, LANES), lambda i: (i % spt, 0)),
            pl.BlockSpec((tm, LANES), lambda i: (i % spt, 0)),
        ],
        out_specs=pl.BlockSpec((tm, heads * 256), lambda i: (i, 0)),
        out_shape=jax.ShapeDtypeStruct((n, heads * 256), BF16),
        compiler_params=_params(("parallel",)),
        name="mla_q_proj",
    )(x, g, wa, lat_g, wb_ext, cos2, sin2)


def _qkv_proj_kernel(x_ref, gkv_ref, gq_ref, wkva_ref, lkv_ref, wkvb_ref, wqa_ref, lq_ref, wqb_ref,
                     cos_ref, sin_ref, k_ref, v_ref, q_ref, *, heads, rank, scale):
    xhat = _rms(x_ref[...])
    h_kv = (xhat * gkv_ref[...]).astype(BF16)
    h_q = (xhat * gq_ref[...]).astype(BF16)
    a = jnp.dot(h_kv, wkva_ref[...], preferred_element_type=F32)
    lat = jnp.dot(h_q, wqa_ref[...], preferred_element_type=F32)
    c_kv = (_rms(a[:, :rank]) * lkv_ref[...]).astype(BF16)
    lat = (_rms(lat) * lq_ref[...]).astype(BF16)
    cos2 = cos_ref[...]
    sin2 = sin_ref[...]
    k_rope = _rope128(a[:, rank:], cos2, sin2).astype(BF16)
    kv = jnp.dot(c_kv, wkvb_ref[...], preferred_element_type=F32)
    q = jnp.dot(lat, wqb_ref[...], preferred_element_type=F32)
    hd = heads * QK_NOPE_DIM
    for hh in range(heads):
        k_ref[:, hh * 256:hh * 256 + 128] = kv[:, hh * 128:(hh + 1) * 128].astype(BF16)
        k_ref[:, hh * 256 + 128:(hh + 1) * 256] = k_rope
    v_ref[...] = kv[:, hd:].astype(BF16)
    for hh in range(heads):
        q_ref[:, hh * 256:hh * 256 + 128] = (q[:, hh * 256:hh * 256 + 128] * scale).astype(BF16)
        r = _rope128(q[:, hh * 256 + 128:(hh + 1) * 256], cos2, sin2)
        q_ref[:, hh * 256 + 128:(hh + 1) * 256] = (r * scale).astype(BF16)


def _qkv_proj(x, g_kv, g_q, wkva_ext, lat_kv, wkvb_split, wqa, lat_q, wqb_ext, cos2, sin2, seq, tm):
    n, d = x.shape
    heads = d // 128
    spt = seq // tm
    kern = functools.partial(_qkv_proj_kernel, heads=heads, rank=lat_kv.shape[1], scale=_Q_SCALE)
    full = lambda a: pl.BlockSpec(a.shape, lambda i: (0, 0))
    rope = pl.BlockSpec((tm, LANES), lambda i: (i % spt, 0))
    wide = pl.BlockSpec((tm, heads * 256), lambda i: (i, 0))
    return pl.pallas_call(
        kern,
        grid=(n // tm,),
        in_specs=[pl.BlockSpec((tm, d), lambda i: (i, 0)), full(g_kv), full(g_q), full(wkva_ext),
                  full(lat_kv), full(wkvb_split), full(wqa), full(lat_q), full(wqb_ext), rope, rope],
        out_specs=[wide, pl.BlockSpec((tm, heads * 128), lambda i: (i, 0)), wide],
        out_shape=[jax.ShapeDtypeStruct((n, heads * 256), BF16),
                   jax.ShapeDtypeStruct((n, heads * 128), BF16),
                   jax.ShapeDtypeStruct((n, heads * 256), BF16)],
        compiler_params=_params(("parallel",)),
        name="mla_qkv_proj",
    )(x, g_kv, g_q, wkva_ext, lat_kv, wkvb_split, wqa, lat_q, wqb_ext, cos2, sin2)


def _attn_kernel(q_ref, k_ref, v_ref, o_ref, *, tq, tk):
    seq = q_ref.shape[0]
    row = lax.broadcasted_iota(jnp.int32, (tq, tk), 0)
    col = lax.broadcasted_iota(jnp.int32, (tq, tk), 1)
    for qi in range(seq // tq):
        q = q_ref[qi * tq:(qi + 1) * tq, :]
        m = jnp.full((tq, 1), -jnp.inf, F32)
        l = jnp.zeros((tq, 1), F32)
        acc = jnp.zeros((tq, V_DIM), F32)
        for j in range((qi + 1) * tq // tk):
            k = k_ref[j * tk:(j + 1) * tk, :]
            s = lax.dot_general(q, k, (((1,), (1,)), ((), ())), preferred_element_type=F32)
            if (j + 1) * tk > qi * tq + 1:
                s = jnp.where(row + qi * tq >= col + j * tk, s, -jnp.inf)
            m_new = jnp.maximum(m, jnp.max(s, axis=1, keepdims=True))
            p = jnp.exp2(s - m_new)
            alpha = jnp.exp2(m - m_new)
            l = alpha * l + jnp.sum(p, axis=1, keepdims=True)
            pv = jnp.dot(p.astype(BF16), v_ref[j * tk:(j + 1) * tk, :],
                         preferred_element_type=F32)
            acc = alpha * acc + pv
            m = m_new
        o_ref[qi * tq:(qi + 1) * tq, :] = (acc / l).astype(o_ref.dtype)


def _attention(q, k, v, batch, seq, tq, tk):
    n = q.shape[0]
    heads = v.shape[1] // V_DIM
    kern = functools.partial(_attn_kernel, tq=tq, tk=tk)
    return pl.pallas_call(
        kern,
        grid=(batch, heads),
        in_specs=[
            pl.BlockSpec((seq, 256), lambda b, h: (b, h)),
            pl.BlockSpec((seq, 256), lambda b, h: (b, h)),
            pl.BlockSpec((seq, V_DIM), lambda b, h: (b, h)),
        ],
        out_specs=pl.BlockSpec((seq, V_DIM), lambda b, h: (b, h)),
        out_shape=jax.ShapeDtypeStruct((n, heads * V_DIM), BF16),
        compiler_params=_params(("parallel", "parallel")),
        name="mla_attention",
    )(q, k, v)


def _attn_out_kernel(o_ref, x_ref, w_ref, g_ref, xo_ref):
    m = jnp.dot(o_ref[...], w_ref[...], preferred_element_type=F32)
    xo_ref[...] = x_ref[...] + _rms(m) * g_ref[...]


def _attn_out(o, x, w, g, tm):
    n, d = x.shape
    row = pl.BlockSpec((tm, d), lambda i: (i, 0))
    return pl.pallas_call(
        _attn_out_kernel,
        grid=(n // tm,),
        in_specs=[pl.BlockSpec((tm, o.shape[1]), lambda i: (i, 0)), row,
                  pl.BlockSpec(w.shape, lambda i: (0, 0)), pl.BlockSpec((1, d), lambda i: (0, 0))],
        out_specs=row,
        out_shape=jax.ShapeDtypeStruct((n, d), F32),
        compiler_params=_params(("parallel",)),
        name="mla_out_proj",
    )(o, x, w, g)


def _router_kernel(x_ref, g_ref, wcat_ref, h_ref, ri_ref, rw_ref, cnt_ref, base_ref,
                   *, experts, slabs):
    tm = x_ref.shape[0]

    @pl.when(pl.program_id(0) == 0)
    def _():
        base_ref[...] = jnp.zeros_like(base_ref)

    h = _rms(x_ref[...]) * g_ref[...]
    for f in range(slabs):
        h_ref[pl.ds(f, tm, stride=slabs), :] = h[:, f * LANES:(f + 1) * LANES]

    h_hi = h.astype(BF16)
    h_lo = (h - h_hi.astype(F32)).astype(BF16)
    hi = jnp.dot(h_hi, wcat_ref[...], preferred_element_type=F32)
    logits = (hi[:, :LANES] + hi[:, LANES:]
              + jnp.dot(h_lo, wcat_ref[:, :LANES], preferred_element_type=F32))

    lane = lax.broadcasted_iota(jnp.int32, (tm, LANES), 1)
    lg = jnp.where(lane < experts, logits, -jnp.inf)
    m1 = jnp.max(lg, axis=1, keepdims=True)
    i1 = jnp.min(jnp.where(lg == m1, lane, LANES), axis=1, keepdims=True)
    oh1 = lane == i1
    lg2 = jnp.where(oh1, -jnp.inf, lg)
    m2 = jnp.max(lg2, axis=1, keepdims=True)
    i2 = jnp.min(jnp.where(lg2 == m2, lane, LANES), axis=1, keepdims=True)
    oh2 = lane == i2
    e2 = jnp.exp(m2 - m1)
    w1 = 1.0 / (1.0 + e2)
    w2 = e2 / (1.0 + e2)

    cnt = oh1.astype(F32) + oh2.astype(F32)
    r_i = lax.broadcasted_iota(jnp.int32, (tm, tm), 0)
    c_i = lax.broadcasted_iota(jnp.int32, (tm, tm), 1)
    tri = (c_i < r_i).astype(BF16)
    before = jnp.dot(tri, cnt.astype(BF16), preferred_element_type=F32) + base_ref[0:1, :]
    rank1 = jnp.sum(jnp.where(oh1, before, 0.0), axis=1, keepdims=True).astype(jnp.int32)
    rank2 = jnp.sum(jnp.where(oh2, before, 0.0), axis=1, keepdims=True).astype(jnp.int32)
    total = base_ref[0:1, :] + jnp.sum(cnt, axis=0, keepdims=True)
    base_ref[...] = jnp.broadcast_to(total, base_ref.shape)
    cnt_ref[...] = jnp.broadcast_to(total, cnt_ref.shape).astype(jnp.int32)

    ri_ref[...] = jnp.where(lane == 0, i1, jnp.where(lane == 1, i2,
                            jnp.where(lane == 2, rank1, jnp.where(lane == 3, rank2, 0))))
    rw_ref[...] = jnp.where(lane == 0, w1, jnp.where(lane == 1, w2, 0.0))


def _router(x, g, w_router, tm):
    n, d = x.shape
    experts = w_router.shape[1]
    slabs = d // LANES
    w_pad = jnp.zeros((d, LANES), F32).at[:, :experts].set(w_router)
    w_hi = w_pad.astype(BF16)
    w_lo = (w_pad - w_hi.astype(F32)).astype(BF16)
    w_cat = jnp.concatenate([w_hi, w_lo], axis=1)
    kern = functools.partial(_router_kernel, experts=experts, slabs=slabs)
    return pl.pallas_call(
        kern,
        grid=(n // tm,),
        in_specs=[pl.BlockSpec((tm, d), lambda i: (i, 0)), pl.BlockSpec((1, d), lambda i: (0, 0)),
                  pl.BlockSpec((d, 2 * LANES), lambda i: (0, 0))],
        out_specs=[pl.BlockSpec((tm * slabs, LANES), lambda i: (i, 0)),
                   pl.BlockSpec((tm, LANES), lambda i: (i, 0)),
                   pl.BlockSpec((tm, LANES), lambda i: (i, 0)),
                   pl.BlockSpec((8, LANES), lambda i: (0, 0))],
        out_shape=[jax.ShapeDtypeStruct((n * slabs, LANES), F32),
                   jax.ShapeDtypeStruct((n, LANES), jnp.int32),
                   jax.ShapeDtypeStruct((n, LANES), F32),
                   jax.ShapeDtypeStruct((8, LANES), jnp.int32)],
        scratch_shapes=[pltpu.VMEM((8, LANES), F32)],
        compiler_params=_params(("arbitrary",)),
        name="moe_router",
    )(x, g, w_cat)


def _pitch(slabs):
    groups = slabs // SUBLANES
    return slabs if groups % 2 == 1 else slabs + SUBLANES


def _row_copy(src_ref, dst_ref, sem, src_row, dst_row, slabs):
    pitch = _pitch(slabs)
    return pltpu.make_async_copy(
        src_ref.at[pl.ds(pl.multiple_of(src_row * slabs, slabs), slabs), :],
        dst_ref.at[pl.ds(pl.multiple_of(dst_row * pitch, SUBLANES), slabs), :],
        sem)


def _gather_rows(src_ref, dst_ref, sem, idx_ref, base, count, slabs, wait):
    def body(q, carry):
        for j in range(GATHER_UNROLL):
            r = q * GATHER_UNROLL + j
            cp = _row_copy(src_ref, dst_ref, sem, idx_ref[base + r], r, slabs)
            if wait:
                cp.wait()
            else:
                cp.start(priority=j % 2)
        return carry

    assert count % GATHER_UNROLL == 0
    lax.fori_loop(0, count // GATHER_UNROLL, body, 0)


def _dispatch_kernel(tok_ref, live_ref, h_ref, o_ref, gbuf_ref, sem, *, slabs):
    i = pl.program_id(0)
    last = pl.num_programs(0) - 1
    tr = o_ref.shape[0]
    slot = i % GATHER_DEPTH

    def fetch(tile, wait):
        s = tile % GATHER_DEPTH
        _gather_rows(h_ref, gbuf_ref.at[s], sem.at[s], tok_ref, tile * tr, tr, slabs, wait)

    def fetch_if_live(tile):
        @pl.when(jnp.logical_and(tile <= last, live_ref[jnp.minimum(tile, last)] == 1))
        def _():
            fetch(tile, False)

    @pl.when(i == 0)
    def _():
        for t in range(GATHER_DEPTH - 1):
            fetch_if_live(t)

    fetch_if_live(i + GATHER_DEPTH - 1)

    @pl.when(live_ref[i] == 1)
    def _():
        fetch(i, True)
        for f in range(slabs):
            o_ref[:, f * LANES:(f + 1) * LANES] = (
                gbuf_ref[slot, pl.ds(f, tr, stride=_pitch(slabs)), :].astype(BF16))

    @pl.when(live_ref[i] == 0)
    def _():
        o_ref[...] = jnp.zeros_like(o_ref)


def _dispatch(tok_of_slot, sub_live, h_slabs, d, tr):
    rows = tok_of_slot.shape[0]
    slabs = d // LANES
    grid_spec = pltpu.PrefetchScalarGridSpec(
        num_scalar_prefetch=2,
        grid=(rows // tr,),
        in_specs=[pl.BlockSpec(memory_space=pl.ANY)],
        out_specs=pl.BlockSpec((tr, d), lambda i, tok, live: (i, 0)),
        scratch_shapes=[pltpu.VMEM((GATHER_DEPTH, tr * _pitch(slabs), LANES), F32),
                        pltpu.SemaphoreType.DMA((GATHER_DEPTH,))],
    )
    return pl.pallas_call(
        functools.partial(_dispatch_kernel, slabs=slabs),
        grid_spec=grid_spec,
        out_shape=jax.ShapeDtypeStruct((rows, d), BF16),
        compiler_params=_params(("arbitrary",)),
        name="moe_dispatch",
    )(tok_of_slot, sub_live, h_slabs)


def _expert_ffn_kernel(pe_ref, ps_ref, pn_ref, pt_ref, used_ref, xs_ref, wg_ref, wu_ref, wd_ref, y_ref,
                       xv_ref, acc_ref, stage_ref, sem, *, tr, slabs):
    p = pl.program_id(0)
    c = pl.program_id(1)
    last = pl.num_programs(1) - 1
    nsub = pn_ref[p]
    sub0 = ps_ref[p]
    live = nsub > 0

    def rows_of(r, rows=tr):
        return pl.ds(pl.multiple_of(r * tr, tr), rows)

    def x_copy(r):
        return pltpu.make_async_copy(
            xs_ref.at[pl.ds(pl.multiple_of((sub0 + r) * tr, tr), tr), :],
            xv_ref.at[rows_of(r), :], sem.at[0])

    def x_start(r, carry):
        x_copy(r).start()
        return carry

    def x_wait(r, carry):
        x_copy(r).wait()
        return carry

    @pl.when(jnp.logical_and(live, c == 0))
    def _():
        lax.fori_loop(0, nsub, x_start, 0)
        lax.fori_loop(0, nsub, x_wait, 0)

    def hidden(r, rows):
        x = xv_ref[rows_of(r, rows), :]
        g = jnp.dot(x, wg_ref[...].astype(BF16), preferred_element_type=F32)
        u = jnp.dot(x, wu_ref[...].astype(BF16), preferred_element_type=F32)
        return (g * jax.nn.sigmoid(g) * u).astype(BF16)

    def y_copy(sub, slot):
        return pltpu.make_async_copy(
            stage_ref.at[slot],
            y_ref.at[pl.ds(pl.multiple_of(sub * tr * slabs, tr * slabs), tr * slabs), :],
            sem.at[1 + slot])

    def first_apply(r, slot, part):
        acc_ref[rows_of(r, part.shape[0]), :] = part

    def middle_apply(r, slot, part):
        acc_ref[rows_of(r, part.shape[0]), :] += part

    def final_apply(r, slot, part):
        rows = part.shape[0]
        res = acc_ref[rows_of(r, rows), :] + part

        @pl.when(r >= 2)
        def _():
            y_copy(sub0 + r - 2, slot).wait()

        for f in range(slabs):
            stage_ref[slot, pl.ds(f, rows, stride=slabs), :] = res[:, f * LANES:(f + 1) * LANES]
        if rows < tr:
            stage_ref[slot, rows * slabs:, :] = jnp.zeros(((tr - rows) * slabs, LANES), F32)
        y_copy(sub0 + r, slot).start()

    def for_each_subtile(apply, group):
        def run(first, count, rows=tr):
            hids = [hidden(first + t, rows) for t in range(count)]
            for t in range(count):
                part = jnp.dot(hids[t], wd_ref[...].astype(BF16), preferred_element_type=F32)
                apply(first + t, t % 2, part)

        def body(q, carry):
            run(group * q, group)
            return carry

        lax.fori_loop(0, nsub // group, body, 0)
        rest = nsub % group
        size = group // 2
        while size > 1:
            @pl.when((rest // size) % 2 == 1)
            def _(size=size):
                run(nsub - rest % (2 * size), size)
            size //= 2
        short = pt_ref[p] <= tr // 2

        @pl.when(jnp.logical_and(rest % 2 == 1, short))
        def _():
            run(nsub - 1, 1, tr // 2)

        @pl.when(jnp.logical_and(rest % 2 == 1, jnp.logical_not(short)))
        def _():
            run(nsub - 1, 1)

    @pl.when(live)
    def _():
        @pl.when(c == 0)
        def _():
            for_each_subtile(first_apply, 2)

        @pl.when(jnp.logical_and(c > 0, c < last))
        def _():
            for_each_subtile(middle_apply, 4)

        @pl.when(c == last)
        def _():
            for_each_subtile(final_apply, 2)

            @pl.when(nsub >= 2)
            def _():
                y_copy(sub0 + nsub - 2, nsub % 2).wait()

            y_copy(sub0 + nsub - 1, (nsub - 1) % 2).wait()

    @pl.when(jnp.logical_and(p == pl.num_programs(0) - 1, c == last))
    def _():
        stage_ref[0] = jnp.zeros(stage_ref.shape[1:], F32)

        def zero(s, carry):
            y_copy(s, 0).start()
            y_copy(s, 0).wait()
            return carry

        lax.fori_loop(used_ref[0], y_ref.shape[0] // (tr * slabs), zero, 0)


def _expert_ffn(xs, pass_table, used, w_gu, w_down, layer, tr, cap_sub, tc):
    rows, d = xs.shape
    fe = w_down.shape[2]
    nc = fe // tc
    assert nc >= 2
    slabs = d // LANES
    last = nc - 1

    def chunk(c, pn, p):
        live = jnp.minimum(pn[p], 1)
        return c * live + last * (1 - live)

    grid_spec = pltpu.PrefetchScalarGridSpec(
        num_scalar_prefetch=5,
        grid=(pass_table[0].shape[0], nc),
        in_specs=[
            pl.BlockSpec(memory_space=pl.ANY),
            pl.BlockSpec((None, None, d, tc),
                         lambda p, c, pe, ps, pn, pt, us: (layer, pe[p], 0, chunk(c, pn, p))),
            pl.BlockSpec((None, None, d, tc),
                         lambda p, c, pe, ps, pn, pt, us: (layer, pe[p], 0, nc + chunk(c, pn, p))),
            pl.BlockSpec((None, None, tc, d),
                         lambda p, c, pe, ps, pn, pt, us: (layer, pe[p], chunk(c, pn, p), 0)),
        ],
        out_specs=pl.BlockSpec(memory_space=pl.ANY),
        scratch_shapes=[pltpu.VMEM((cap_sub * tr, d), BF16), pltpu.VMEM((cap_sub * tr, d), F32),
                        pltpu.VMEM((2, tr * slabs, LANES), F32), pltpu.SemaphoreType.DMA((3,))],
    )
    return pl.pallas_call(
        functools.partial(_expert_ffn_kernel, tr=tr, slabs=slabs),
        grid_spec=grid_spec,
        out_shape=jax.ShapeDtypeStruct((rows * slabs, LANES), F32),
        compiler_params=_params(("arbitrary", "arbitrary")),
        name="moe_expert_swiglu",
    )(*pass_table, used, xs, w_gu, w_gu, w_down)


def _combine_kernel(s1_ref, s2_ref, y_ref, rw_ref, x_ref, g_ref, o_ref, b1_ref, b2_ref, sem,
                    *, slabs):
    i = pl.program_id(0)
    last = pl.num_programs(0) - 1
    tm = x_ref.shape[0]
    slot = i % GATHER_DEPTH

    def fetch(tile, wait):
        s = tile % GATHER_DEPTH
        _gather_rows(y_ref, b1_ref.at[s], sem.at[0, s], s1_ref, tile * tm, tm, slabs, wait)
        _gather_rows(y_ref, b2_ref.at[s], sem.at[1, s], s2_ref, tile * tm, tm, slabs, wait)

    @pl.when(i == 0)
    def _():
        for t in range(GATHER_DEPTH - 1):
            @pl.when(t <= last)
            def _(t=t):
                fetch(t, False)

    @pl.when(i + GATHER_DEPTH - 1 <= last)
    def _():
        fetch(i + GATHER_DEPTH - 1, False)

    fetch(i, True)

    rw = rw_ref[...]
    w1 = rw[:, 0:1]
    w2 = rw[:, 1:2]
    parts = []
    ssq = jnp.zeros((tm, 1), F32)
    for f in range(slabs):
        m = (w1 * b1_ref[slot, pl.ds(f, tm, stride=_pitch(slabs)), :]
             + w2 * b2_ref[slot, pl.ds(f, tm, stride=_pitch(slabs)), :])
        ssq = ssq + jnp.sum(m * m, axis=1, keepdims=True)
        parts.append(m)
    inv = lax.rsqrt(ssq / (slabs * LANES) + NORM_EPS)
    for f in range(slabs):
        sl = slice(f * LANES, (f + 1) * LANES)
        o_ref[:, sl] = x_ref[:, sl] + parts[f] * inv * g_ref[:, sl]


def _combine(slot1, slot2, y_slabs, rw, x, g, tm):
    n, d = x.shape
    slabs = d // LANES
    grid_spec = pltpu.PrefetchScalarGridSpec(
        num_scalar_prefetch=2,
        grid=(n // tm,),
        in_specs=[
            pl.BlockSpec(memory_space=pl.ANY),
            pl.BlockSpec((tm, LANES), lambda i, s1, s2: (i, 0)),
            pl.BlockSpec((tm, d), lambda i, s1, s2: (i, 0)),
            pl.BlockSpec((1, d), lambda i, s1, s2: (0, 0)),
        ],
        out_specs=pl.BlockSpec((tm, d), lambda i, s1, s2: (i, 0)),
        scratch_shapes=[pltpu.VMEM((GATHER_DEPTH, tm * _pitch(slabs), LANES), F32),
                        pltpu.VMEM((GATHER_DEPTH, tm * _pitch(slabs), LANES), F32),
                        pltpu.SemaphoreType.DMA((2, GATHER_DEPTH))],
    )
    return pl.pallas_call(
        functools.partial(_combine_kernel, slabs=slabs),
        grid_spec=grid_spec,
        out_shape=jax.ShapeDtypeStruct((n, d), F32),
        compiler_params=_params(("arbitrary",)),
        name="moe_combine",
    )(slot1, slot2, y_slabs, rw, x, g)


def _moe_ffn(x, g_in, g_out, w_router, w_gu, w_down, layer, tiles):
    n, d = x.shape
    experts = w_router.shape[1]
    tr = tiles["expert_rows"]
    cap_sub = tiles["expert_cap"] // tr
    h_slabs, ri, rw, cnt = _router(x, g_in, w_router, tiles["router_rows"])

    counts = cnt[0, :experts]
    nsub = (counts + tr - 1) // tr
    sub_end = jnp.cumsum(nsub)
    sub_start = sub_end - nsub
    slot1 = (sub_start[ri[:, 0]] * tr + ri[:, 2]).astype(jnp.int32)
    slot2 = (sub_start[ri[:, 1]] * tr + ri[:, 3]).astype(jnp.int32)
    rows = TOP_K * n + experts * tr
    n_sub = rows // tr
    token = jnp.arange(n, dtype=jnp.int32)
    tok_of_slot = jnp.zeros((rows,), jnp.int32).at[jnp.concatenate([slot1, slot2])].set(
        jnp.concatenate([token, token]), unique_indices=True)
    sub_live = (jnp.arange(n_sub, dtype=jnp.int32) < sub_end[-1]).astype(jnp.int32)

    n_pass = experts + n_sub // cap_sub
    passes = (nsub + cap_sub - 1) // cap_sub
    pass_end = jnp.cumsum(passes)
    pidx = jnp.arange(n_pass, dtype=jnp.int32)
    pe = jnp.minimum(jnp.searchsorted(pass_end, pidx, side="right"), experts - 1)
    local = pidx - (pass_end[pe] - passes[pe])
    p_live = pidx < pass_end[-1]
    pass_nsub = jnp.where(p_live, jnp.minimum(cap_sub, nsub[pe] - local * cap_sub), 0)
    pass_sub0 = jnp.where(p_live, sub_start[pe] + local * cap_sub, 0)
    last_pe = jnp.max(jnp.where(passes > 0, jnp.arange(experts), 0))
    pass_expert = jnp.where(p_live, pe, last_pe)
    pass_tokens = jnp.minimum(cap_sub * tr, counts[pe] - local * cap_sub * tr)
    pass_tail = jnp.where(p_live, pass_tokens - (pass_nsub - 1) * tr, 0)
    pass_table = tuple(a.astype(jnp.int32) for a in (pass_expert, pass_sub0, pass_nsub, pass_tail))

    xs = _dispatch(tok_of_slot, sub_live, h_slabs, d, tr)
    y_slabs = _expert_ffn(xs, pass_table, sub_end[-1:].astype(jnp.int32), w_gu, w_down, layer, tr,
                          cap_sub, tiles["expert_chunk"])
    return _combine(slot1, slot2, y_slabs, rw, x, g_out, tiles["combine_rows"])


def _tiles(n, seq, d, ffn, expert_ffn):
    pick = lambda pref, dim: min(pref, dim)
    return {
        "conv_rows": pick(256, seq), "conv_cols": pick(512, d),
        "ln_rows": pick(256, n),
        "ffn_rows": pick(512, n), "ffn_chunk": pick(512, ffn),
        "proj_rows": pick(256, seq),
        "attn_q": pick(512, seq), "attn_k": pick(512, seq),
        "out_rows": pick(256, n),
        "router_rows": pick(512, n),
        "expert_rows": pick(256, n), "expert_cap": pick(2304, TOP_K * n),
        "expert_chunk": pick(256, expert_ffn),
        "combine_rows": pick(256, n),
    }


def _rope_tables(seq):
    pos = jnp.arange(seq, dtype=F32)
    inv_freq = ROPE_THETA ** (-jnp.arange(0, QK_ROPE_DIM, 2, dtype=F32) / QK_ROPE_DIM)
    ang = pos[:, None] * inv_freq[None, :]
    cos, sin = jnp.cos(ang), jnp.sin(ang)
    zeros = jnp.zeros((seq, LANES - QK_ROPE_DIM), F32)
    return (jnp.concatenate([cos, cos, zeros], axis=1),
            jnp.concatenate([-sin, sin, zeros], axis=1))


def _swap_halves(w):
    half = w.shape[-1] // 2
    return jnp.concatenate([w[..., half:], w[..., :half]], axis=-1)


def kernel(x, norm_g, conv_w_pw1, conv_b_pw1, conv_w_dw, conv_b_dw, conv_ln_g, conv_ln_b,
           conv_w_pw2, ffn_w_gu, ffn_w_down, moe_w_router, moe_w_gu, moe_w_down,
           kv_in_g, kv_w_a, kv_latent_g, kv_w_b, attn_w_q_a, attn_q_latent_g,
           attn_w_q_b, attn_w_o):
    batch, seq, d = x.shape
    n = batch * seq
    depth = norm_g.shape[0]
    n_conv = depth // 2
    heads = d // 128
    kv_rank = kv_latent_g.shape[0]
    tiles = _tiles(n, seq, d, ffn_w_down.shape[1], moe_w_down.shape[2])
    cos2, sin2 = _rope_tables(seq)
    vec = lambda v: v.reshape(1, -1)

    xs = x.reshape(n, d)
    k = v = None
    for i in range(depth):
        if i < n_conv:
            c = _pw1_dwconv(xs, vec(norm_g[i, 0]), conv_w_pw1[i].astype(BF16), vec(conv_b_pw1[i]),
                            conv_w_dw[i], vec(conv_b_dw[i]), seq, tiles["conv_rows"],
                            tiles["conv_cols"])
            xs = _ln_pw2(c, xs, vec(conv_ln_g[i]), vec(conv_ln_b[i]), conv_w_pw2[i].astype(BF16),
                         vec(norm_g[i, 1]), tiles["ln_rows"])
        else:
            j = i - n_conv
            q_rank = attn_w_q_a.shape[2]
            wqb = attn_w_q_b[j].reshape(q_rank, heads, QK_NOPE_DIM + QK_ROPE_DIM)
            rope_w = wqb[:, :, QK_NOPE_DIM:]
            wqb_ext = jnp.concatenate([wqb, _swap_halves(rope_w)], axis=2)
            wqb_ext = wqb_ext.reshape(q_rank, heads * 256).astype(BF16)
            q_args = (attn_w_q_a[j].astype(BF16), vec(attn_q_latent_g[j]), wqb_ext, cos2, sin2,
                      seq, tiles["proj_rows"])
            if j == 0:
                rope_w = kv_w_a[:, kv_rank:]
                wa_ext = jnp.concatenate([kv_w_a, _swap_halves(rope_w)], axis=1).astype(BF16)
                wb = kv_w_b.reshape(kv_rank, heads, QK_NOPE_DIM + V_DIM)
                wb_split = jnp.concatenate(
                    [wb[:, :, :QK_NOPE_DIM].reshape(kv_rank, heads * QK_NOPE_DIM),
                     wb[:, :, QK_NOPE_DIM:].reshape(kv_rank, heads * V_DIM)], axis=1).astype(BF16)
                k, v, q = _qkv_proj(xs, vec(kv_in_g), vec(norm_g[i, 0]), wa_ext, vec(kv_latent_g),
                                    wb_split, *q_args)
            else:
                q = _q_proj(xs, vec(norm_g[i, 0]), *q_args)
            o = _attention(q, k, v, batch, seq, tiles["attn_q"], tiles["attn_k"])
            xs = _attn_out(o, xs, attn_w_o[j].astype(BF16), vec(norm_g[i, 1]), tiles["out_rows"])
        if i % 2 == 0:
            xs = _dense_ffn(xs, vec(norm_g[i, 2]), vec(norm_g[i, 3]), ffn_w_gu[i // 2].astype(BF16),
                            ffn_w_down[i // 2].astype(BF16), tiles["ffn_rows"], tiles["ffn_chunk"])
        else:
            xs = _moe_ffn(xs, vec(norm_g[i, 2]), vec(norm_g[i, 3]), moe_w_router[i // 2],
                          moe_w_gu, moe_w_down, i // 2, tiles)
    return xs.reshape(batch, seq, d)
```

```python
import functools

import jax
import jax.numpy as jnp
from jax import lax
from jax.experimental import pallas as pl
from jax.experimental.pallas import tpu as pltpu

NORM_EPS = 1e-6
ROPE_THETA = 10000.0
QK_NOPE_DIM = 128
QK_ROPE_DIM = 64
V_DIM = 128
TOP_K = 2
_Q_SCALE = float(QK_NOPE_DIM + QK_ROPE_DIM) ** -0.5 * 1.4426950408889634

LANES = 128
SUBLANES = 8
CONV_HALO = 32
GATHER_UNROLL = 8
GATHER_DEPTH = 3
VMEM_LIMIT = 56 * 1024 * 1024

F32 = jnp.float32
BF16 = jnp.bfloat16


def _params(semantics):
    return pltpu.CompilerParams(dimension_semantics=semantics, vmem_limit_bytes=VMEM_LIMIT)


def _rms(x):
    return x * lax.rsqrt(jnp.mean(x * x, axis=-1, keepdims=True) + NORM_EPS)


_HIGH_HALF = 0xFFFF0000


def _pack_pairs(x, f):
    half = x.shape[1] // (2 * LANES)
    lo = x[:, f * LANES:(f + 1) * LANES].astype(BF16).astype(F32)
    hi = x[:, (f + half) * LANES:(f + half + 1) * LANES].astype(BF16).astype(F32)
    lo_bits = lax.shift_right_logical(lax.bitcast_convert_type(lo, jnp.uint32), jnp.uint32(16))
    hi_bits = lax.bitcast_convert_type(hi, jnp.uint32) & jnp.uint32(_HIGH_HALF)
    return hi_bits | lo_bits


def _unpack_pairs(w):
    lo = lax.bitcast_convert_type(lax.shift_left(w, jnp.uint32(16)), F32)
    hi = lax.bitcast_convert_type(w & jnp.uint32(_HIGH_HALF), F32)
    return lo, hi


def _pw1_dwconv_kernel(x_ref, g_ref, w_ref, b_ref, wdw_ref, bdw_ref, o_ref,
                       xn_ref, carry_ref, sh_ref, *, tiles_per_seq, tn, rb, cb):
    tm, d = x_ref.shape
    width = wdw_ref.shape[0]
    ext = tm + CONV_HALO
    base = CONV_HALO - (width - 1)
    first = (pl.program_id(0) % tiles_per_seq) == 0
    xn_ref[...] = (_rms(x_ref[...]) * g_ref[...]).astype(BF16)

    @pl.when(pl.program_id(0) == 0)
    def _():
        carry_ref[...] = jnp.zeros_like(carry_ref)

    for j in range(d // tn):
        cols = slice(j * tn, (j + 1) * tn)
        gate_cols = slice(d + j * tn, d + (j + 1) * tn)
        a = jnp.dot(xn_ref[...], w_ref[:, cols], preferred_element_type=F32) + b_ref[:, cols]
        g = jnp.dot(xn_ref[...], w_ref[:, gate_cols], preferred_element_type=F32) + b_ref[:, gate_cols]
        u = a * jax.nn.sigmoid(g)
        sh = sh_ref.at[j % 2]
        halo = carry_ref[:, cols]
        sh[0, 0:CONV_HALO, :] = jnp.where(first, jnp.zeros_like(halo), halo)
        sh[0, CONV_HALO:ext, :] = u
        carry_ref[:, cols] = u[tm - CONV_HALO:, :]
        for s in range(1, SUBLANES):
            sh[s, 0:ext - SUBLANES, :] = sh[0, s:ext - SUBLANES + s, :]
        for r0 in range(0, tm, rb):
            for c0 in range(0, tn, cb):
                acc = None
                for k in range(width):
                    s = (base + k) % SUBLANES
                    off = (base + k) - s
                    term = (wdw_ref[k:k + 1, j * tn + c0:j * tn + c0 + cb]
                            * sh[s, r0 + off:r0 + off + rb, c0:c0 + cb])
                    acc = term if acc is None else acc + term
                o_ref[r0:r0 + rb, j * tn + c0:j * tn + c0 + cb] = (
                    acc + bdw_ref[:, j * tn + c0:j * tn + c0 + cb]).astype(o_ref.dtype)


def _pw1_dwconv(x, g, w, b, w_dw, b_dw, seq, tm, tn):
    n, d = x.shape
    width = w_dw.shape[0]
    assert width - 1 <= CONV_HALO <= tm
    kern = functools.partial(_pw1_dwconv_kernel, tiles_per_seq=seq // tm, tn=tn,
                             rb=min(64, tm), cb=min(256, tn))
    const = lambda shape: pl.BlockSpec(shape, lambda i: (0, 0), pipeline_mode=pl.Buffered(1))
    return pl.pallas_call(
        kern,
        grid=(n // tm,),
        in_specs=[pl.BlockSpec((tm, d), lambda i: (i, 0)), const((1, d)), const(w.shape),
                  const(b.shape), const((width, d)), const((1, d))],
        out_specs=pl.BlockSpec((tm, d), lambda i: (i, 0)),
        out_shape=jax.ShapeDtypeStruct((n, d), BF16),
        scratch_shapes=[pltpu.VMEM((tm, d), BF16), pltpu.VMEM((CONV_HALO, d), F32),
                        pltpu.VMEM((2, SUBLANES, tm + CONV_HALO, tn), F32)],
        compiler_params=_params(("arbitrary",)),
        name="conv_pw1_dwconv",
    )(x, g, w, b, w_dw, b_dw)


def _ln_pw2_kernel(c_ref, x_ref, lg_ref, lb_ref, w_ref, g_ref, o_ref):
    c = c_ref[...].astype(F32)
    mu = jnp.mean(c, axis=-1, keepdims=True)
    cc = c - mu
    y = cc * lax.rsqrt(jnp.mean(cc * cc, axis=-1, keepdims=True) + NORM_EPS)
    y = y * lg_ref[...] + lb_ref[...]
    y = (y * jax.nn.sigmoid(y)).astype(BF16)
    m = jnp.dot(y, w_ref[...], preferred_element_type=F32)
    o_ref[...] = x_ref[...] + _rms(m) * g_ref[...]


def _ln_pw2(c, x, ln_g, ln_b, w, g, tm):
    n, d = x.shape
    row = pl.BlockSpec((tm, d), lambda i: (i, 0))
    vec = pl.BlockSpec((1, d), lambda i: (0, 0))
    return pl.pallas_call(
        _ln_pw2_kernel,
        grid=(n // tm,),
        in_specs=[row, row, vec, vec, pl.BlockSpec((d, d), lambda i: (0, 0)), vec],
        out_specs=row,
        out_shape=jax.ShapeDtypeStruct((n, d), F32),
        compiler_params=_params(("parallel",)),
        name="conv_ln_pw2",
    )(c, x, ln_g, ln_b, w, g)


def _swiglu_chunk(xn_ref, wg_ref, wu_ref, wd_ref, acc_ref, groups):
    rows = xn_ref.shape[0] // groups
    hids = []
    for i in range(groups):
        xn = xn_ref[i * rows:(i + 1) * rows, :]
        g = jnp.dot(xn, wg_ref[...], preferred_element_type=F32)
        u = jnp.dot(xn, wu_ref[...], preferred_element_type=F32)
        hids.append((g * jax.nn.sigmoid(g) * u).astype(BF16))
    for i in range(groups):
        acc_ref[i * rows:(i + 1) * rows, :] += jnp.dot(hids[i], wd_ref[...],
                                                       preferred_element_type=F32)


def _dense_ffn_kernel(x_ref, gi_ref, go_ref, wg_ref, wu_ref, wd_ref, o_ref, xn_ref, acc_ref):
    c = pl.program_id(1)

    @pl.when(c == 0)
    def _():
        xn_ref[...] = (_rms(x_ref[...]) * gi_ref[...]).astype(BF16)
        acc_ref[...] = jnp.zeros_like(acc_ref)

    _swiglu_chunk(xn_ref, wg_ref, wu_ref, wd_ref, acc_ref, groups=2)

    @pl.when(c == pl.num_programs(1) - 1)
    def _():
        o_ref[...] = x_ref[...] + _rms(acc_ref[...]) * go_ref[...]


def _dense_ffn(x, g_in, g_out, w_gu, w_down, tm, tc):
    n, d = x.shape
    f = w_down.shape[0]
    nc = f // tc
    row = pl.BlockSpec((tm, d), lambda i, c: (i, 0))
    vec = pl.BlockSpec((1, d), lambda i, c: (0, 0))
    return pl.pallas_call(
        _dense_ffn_kernel,
        grid=(n // tm, nc),
        in_specs=[
            row, vec, vec,
            pl.BlockSpec((d, tc), lambda i, c: (0, c)),
            pl.BlockSpec((d, tc), lambda i, c: (0, nc + c)),
            pl.BlockSpec((tc, d), lambda i, c: (c, 0)),
        ],
        out_specs=row,
        out_shape=jax.ShapeDtypeStruct((n, d), F32),
        scratch_shapes=[pltpu.VMEM((tm, d), BF16), pltpu.VMEM((tm, d), F32)],
        compiler_params=_params(("parallel", "arbitrary")),
        name="dense_swiglu",
    )(x, g_in, g_out, w_gu, w_gu, w_down)


def _rope128(t, cos2, sin2):
    return t * cos2 + pltpu.roll(t, 64, axis=1) * sin2


def _q_proj_kernel(x_ref, g_ref, wa_ref, lg_ref, wb_ref, cos_ref, sin_ref, q_ref, *, heads, scale):
    h = (_rms(x_ref[...]) * g_ref[...]).astype(BF16)
    lat = jnp.dot(h, wa_ref[...], preferred_element_type=F32)
    lat = (_rms(lat) * lg_ref[...]).astype(BF16)
    q = jnp.dot(lat, wb_ref[...], preferred_element_type=F32)
    cos2 = cos_ref[...]
    sin2 = sin_ref[...]
    for hh in range(heads):
        q_ref[:, hh * 256:hh * 256 + 128] = (q[:, hh * 256:hh * 256 + 128] * scale).astype(BF16)
        r = _rope128(q[:, hh * 256 + 128:(hh + 1) * 256], cos2, sin2)
        q_ref[:, hh * 256 + 128:(hh + 1) * 256] = (r * scale).astype(BF16)


def _q_proj(x, g, wa, lat_g, wb_ext, cos2, sin2, seq, tm):
    n, d = x.shape
    heads = d // 128
    spt = seq // tm
    kern = functools.partial(_q_proj_kernel, heads=heads, scale=_Q_SCALE)
    full = lambda a: pl.BlockSpec(a.shape, lambda i: (0, 0))
    return pl.pallas_call(
        kern,
        grid=(n // tm,),
        in_specs=[
            pl.BlockSpec((tm, d), lambda i: (i, 0)), full(g), full(wa), full(lat_g), full(wb_ext),
            pl.BlockSpec((tm, LANES), lambda i: (i % spt, 0)),
            pl.BlockSpec((tm, LANES), lambda i: (i % spt, 0)),
        ],
        out_specs=pl.BlockSpec((tm, heads * 256), lambda i: (i, 0)),
        out_shape=jax.ShapeDtypeStruct((n, heads * 256), BF16),
        compiler_params=_params(("parallel",)),
        name="mla_q_proj",
    )(x, g, wa, lat_g, wb_ext, cos2, sin2)


def _qkv_proj_kernel(x_ref, gkv_ref, gq_ref, wkva_ref, lkv_ref, wkvb_ref, wqa_ref, lq_ref, wqb_ref,
                     cos_ref, sin_ref, k_ref, v_ref, q_ref, *, heads, rank, scale):
    xhat = _rms(x_ref[...])
    h_kv = (xhat * gkv_ref[...]).astype(BF16)
    h_q = (xhat * gq_ref[...]).astype(BF16)
    a = jnp.dot(h_kv, wkva_ref[...], preferred_element_type=F32)
    lat = jnp.dot(h_q, wqa_ref[...], preferred_element_type=F32)
    c_kv = (_rms(a[:, :rank]) * lkv_ref[...]).astype(BF16)
    lat = (_rms(lat) * lq_ref[...]).astype(BF16)
    cos2 = cos_ref[...]
    sin2 = sin_ref[...]
    k_rope = _rope128(a[:, rank:], cos2, sin2).astype(BF16)
    kv = jnp.dot(c_kv, wkvb_ref[...], preferred_element_type=F32)
    q = jnp.dot(lat, wqb_ref[...], preferred_element_type=F32)
    hd = heads * QK_NOPE_DIM
    for hh in range(heads):
        k_ref[:, hh * 256:hh * 256 + 128] = kv[:, hh * 128:(hh + 1) * 128].astype(BF16)
        k_ref[:, hh * 256 + 128:(hh + 1) * 256] = k_rope
    v_ref[...] = kv[:, hd:].astype(BF16)
    for hh in range(heads):
        q_ref[:, hh * 256:hh * 256 + 128] = (q[:, hh * 256:hh * 256 + 128] * scale).astype(BF16)
        r = _rope128(q[:, hh * 256 + 128:(hh + 1) * 256], cos2, sin2)
        q_ref[:, hh * 256 + 128:(hh + 1) * 256] = (r * scale).astype(BF16)


def _qkv_proj(x, g_kv, g_q, wkva_ext, lat_kv, wkvb_split, wqa, lat_q, wqb_ext, cos2, sin2, seq, tm):
    n, d = x.shape
    heads = d // 128
    spt = seq // tm
    kern = functools.partial(_qkv_proj_kernel, heads=heads, rank=lat_kv.shape[1], scale=_Q_SCALE)
    full = lambda a: pl.BlockSpec(a.shape, lambda i: (0, 0))
    rope = pl.BlockSpec((tm, LANES), lambda i: (i % spt, 0))
    wide = pl.BlockSpec((tm, heads * 256), lambda i: (i, 0))
    return pl.pallas_call(
        kern,
        grid=(n // tm,),
        in_specs=[pl.BlockSpec((tm, d), lambda i: (i, 0)), full(g_kv), full(g_q), full(wkva_ext),
                  full(lat_kv), full(wkvb_split), full(wqa), full(lat_q), full(wqb_ext), rope, rope],
        out_specs=[wide, pl.BlockSpec((tm, heads * 128), lambda i: (i, 0)), wide],
        out_shape=[jax.ShapeDtypeStruct((n, heads * 256), BF16),
                   jax.ShapeDtypeStruct((n, heads * 128), BF16),
                   jax.ShapeDtypeStruct((n, heads * 256), BF16)],
        compiler_params=_params(("parallel",)),
        name="mla_qkv_proj",
    )(x, g_kv, g_q, wkva_ext, lat_kv, wkvb_split, wqa, lat_q, wqb_ext, cos2, sin2)


def _attn_kernel(q_ref, k_ref, v_ref, o_ref, *, tq, tk):
    seq = q_ref.shape[0]
    row = lax.broadcasted_iota(jnp.int32, (tq, tk), 0)
    col = lax.broadcasted_iota(jnp.int32, (tq, tk), 1)
    for qi in range(seq // tq):
        q = q_ref[qi * tq:(qi + 1) * tq, :]
        m = jnp.full((tq, 1), -jnp.inf, F32)
        l = jnp.zeros((tq, 1), F32)
        acc = jnp.zeros((tq, V_DIM), F32)
        for j in range((qi + 1) * tq // tk):
            k = k_ref[j * tk:(j + 1) * tk, :]
            s = lax.dot_general(q, k, (((1,), (1,)), ((), ())), preferred_element_type=F32)
            if (j + 1) * tk > qi * tq + 1:
                s = jnp.where(row + qi * tq >= col + j * tk, s, -jnp.inf)
            m_new = jnp.maximum(m, jnp.max(s, axis=1, keepdims=True))
            p = jnp.exp2(s - m_new)
            alpha = jnp.exp2(m - m_new)
            l = alpha * l + jnp.sum(p, axis=1, keepdims=True)
            pv = jnp.dot(p.astype(BF16), v_ref[j * tk:(j + 1) * tk, :],
                         preferred_element_type=F32)
            acc = alpha * acc + pv
            m = m_new
        o_ref[qi * tq:(qi + 1) * tq, :] = (acc / l).astype(o_ref.dtype)


def _attention(q, k, v, batch, seq, tq, tk):
    n = q.shape[0]
    heads = v.shape[1] // V_DIM
    kern = functools.partial(_attn_kernel, tq=tq, tk=tk)
    return pl.pallas_call(
        kern,
        grid=(batch, heads),
        in_specs=[
            pl.BlockSpec((seq, 256), lambda b, h: (b, h)),
            pl.BlockSpec((seq, 256), lambda b, h: (b, h)),
            pl.BlockSpec((seq, V_DIM), lambda b, h: (b, h)),
        ],
        out_specs=pl.BlockSpec((seq, V_DIM), lambda b, h: (b, h)),
        out_shape=jax.ShapeDtypeStruct((n, heads * V_DIM), BF16),
        compiler_params=_params(("parallel", "parallel")),
        name="mla_attention",
    )(q, k, v)


def _attn_out_kernel(o_ref, x_ref, w_ref, g_ref, xo_ref):
    m = jnp.dot(o_ref[...], w_ref[...], preferred_element_type=F32)
    xo_ref[...] = x_ref[...] + _rms(m) * g_ref[...]


def _attn_out(o, x, w, g, tm):
    n, d = x.shape
    row = pl.BlockSpec((tm, d), lambda i: (i, 0))
    return pl.pallas_call(
        _attn_out_kernel,
        grid=(n // tm,),
        in_specs=[pl.BlockSpec((tm, o.shape[1]), lambda i: (i, 0)), row,
                  pl.BlockSpec(w.shape, lambda i: (0, 0)), pl.BlockSpec((1, d), lambda i: (0, 0))],
        out_specs=row,
        out_shape=jax.ShapeDtypeStruct((n, d), F32),
        compiler_params=_params(("parallel",)),
        name="mla_out_proj",
    )(o, x, w, g)


def _router_kernel(x_ref, g_ref, wcat_ref, h_ref, ri_ref, rw_ref, cnt_ref, base_ref,
                   *, experts, slabs):
    tm = x_ref.shape[0]

    @pl.when(pl.program_id(0) == 0)
    def _():
        base_ref[...] = jnp.zeros_like(base_ref)

    h = _rms(x_ref[...]) * g_ref[...]
    for f in range(slabs):
        h_ref[pl.ds(f, tm, stride=slabs), :] = _pack_pairs(h, f)

    h_hi = h.astype(BF16)
    h_lo = (h - h_hi.astype(F32)).astype(BF16)
    hi = jnp.dot(h_hi, wcat_ref[...], preferred_element_type=F32)
    logits = (hi[:, :LANES] + hi[:, LANES:]
              + jnp.dot(h_lo, wcat_ref[:, :LANES], preferred_element_type=F32))

    lane = lax.broadcasted_iota(jnp.int32, (tm, LANES), 1)
    lg = jnp.where(lane < experts, logits, -jnp.inf)
    m1 = jnp.max(lg, axis=1, keepdims=True)
    i1 = jnp.min(jnp.where(lg == m1, lane, LANES), axis=1, keepdims=True)
    oh1 = lane == i1
    lg2 = jnp.where(oh1, -jnp.inf, lg)
    m2 = jnp.max(lg2, axis=1, keepdims=True)
    i2 = jnp.min(jnp.where(lg2 == m2, lane, LANES), axis=1, keepdims=True)
    oh2 = lane == i2
    e2 = jnp.exp(m2 - m1)
    w1 = 1.0 / (1.0 + e2)
    w2 = e2 / (1.0 + e2)

    cnt = oh1.astype(F32) + oh2.astype(F32)
    r_i = lax.broadcasted_iota(jnp.int32, (tm, tm), 0)
    c_i = lax.broadcasted_iota(jnp.int32, (tm, tm), 1)
    tri = (c_i < r_i).astype(BF16)
    before = jnp.dot(tri, cnt.astype(BF16), preferred_element_type=F32) + base_ref[0:1, :]
    rank1 = jnp.sum(jnp.where(oh1, before, 0.0), axis=1, keepdims=True).astype(jnp.int32)
    rank2 = jnp.sum(jnp.where(oh2, before, 0.0), axis=1, keepdims=True).astype(jnp.int32)
    total = base_ref[0:1, :] + jnp.sum(cnt, axis=0, keepdims=True)
    base_ref[...] = jnp.broadcast_to(total, base_ref.shape)
    cnt_ref[...] = jnp.broadcast_to(total, cnt_ref.shape).astype(jnp.int32)

    ri_ref[...] = jnp.where(lane == 0, i1, jnp.where(lane == 1, i2,
                            jnp.where(lane == 2, rank1, jnp.where(lane == 3, rank2, 0))))
    rw_ref[...] = jnp.where(lane == 0, w1, jnp.where(lane == 1, w2, 0.0))


def _router(x, g, w_router, tm):
    n, d = x.shape
    experts = w_router.shape[1]
    slabs = d // (2 * LANES)
    w_pad = jnp.zeros((d, LANES), F32).at[:, :experts].set(w_router)
    w_hi = w_pad.astype(BF16)
    w_lo = (w_pad - w_hi.astype(F32)).astype(BF16)
    w_cat = jnp.concatenate([w_hi, w_lo], axis=1)
    kern = functools.partial(_router_kernel, experts=experts, slabs=slabs)
    return pl.pallas_call(
        kern,
        grid=(n // tm,),
        in_specs=[pl.BlockSpec((tm, d), lambda i: (i, 0)), pl.BlockSpec((1, d), lambda i: (0, 0)),
                  pl.BlockSpec((d, 2 * LANES), lambda i: (0, 0))],
        out_specs=[pl.BlockSpec((tm * slabs, LANES), lambda i: (i, 0)),
                   pl.BlockSpec((tm, LANES), lambda i: (i, 0)),
                   pl.BlockSpec((tm, LANES), lambda i: (i, 0)),
                   pl.BlockSpec((8, LANES), lambda i: (0, 0))],
        out_shape=[jax.ShapeDtypeStruct((n * slabs, LANES), jnp.uint32),
                   jax.ShapeDtypeStruct((n, LANES), jnp.int32),
                   jax.ShapeDtypeStruct((n, LANES), F32),
                   jax.ShapeDtypeStruct((8, LANES), jnp.int32)],
        scratch_shapes=[pltpu.VMEM((8, LANES), F32)],
        compiler_params=_params(("arbitrary",)),
        name="moe_router",
    )(x, g, w_cat)


def _pitch(slabs):
    groups = slabs // SUBLANES
    return slabs if groups % 2 == 1 else slabs + SUBLANES


def _row_copy(src_ref, dst_ref, sem, src_row, dst_row, slabs):
    pitch = _pitch(slabs)
    return pltpu.make_async_copy(
        src_ref.at[pl.ds(pl.multiple_of(src_row * slabs, slabs), slabs), :],
        dst_ref.at[pl.ds(pl.multiple_of(dst_row * pitch, SUBLANES), slabs), :],
        sem)


def _gather_rows(src_ref, dst_ref, sem, idx_ref, base, count, slabs, wait):
    def body(q, carry):
        for j in range(GATHER_UNROLL):
            r = q * GATHER_UNROLL + j
            cp = _row_copy(src_ref, dst_ref, sem, idx_ref[base + r], r, slabs)
            if wait:
                cp.wait()
            else:
                cp.start(priority=j % 2)
        return carry

    assert count % GATHER_UNROLL == 0
    lax.fori_loop(0, count // GATHER_UNROLL, body, 0)


def _dispatch_kernel(tok_ref, live_ref, h_ref, o_ref, gbuf_ref, sem, *, slabs):
    i = pl.program_id(0)
    last = pl.num_programs(0) - 1
    tr = o_ref.shape[0]
    slot = i % GATHER_DEPTH

    def fetch(tile, wait):
        s = tile % GATHER_DEPTH
        _gather_rows(h_ref, gbuf_ref.at[s], sem.at[s], tok_ref, tile * tr, tr, slabs, wait)

    def fetch_if_live(tile):
        @pl.when(jnp.logical_and(tile <= last, live_ref[jnp.minimum(tile, last)] == 1))
        def _():
            fetch(tile, False)

    @pl.when(i == 0)
    def _():
        for t in range(GATHER_DEPTH - 1):
            fetch_if_live(t)

    fetch_if_live(i + GATHER_DEPTH - 1)

    @pl.when(live_ref[i] == 1)
    def _():
        fetch(i, True)
        for f in range(slabs):
            lo, hi = _unpack_pairs(gbuf_ref[slot, pl.ds(f, tr, stride=_pitch(slabs)), :])
            o_ref[:, f * LANES:(f + 1) * LANES] = lo.astype(BF16)
            o_ref[:, (f + slabs) * LANES:(f + slabs + 1) * LANES] = hi.astype(BF16)

    @pl.when(live_ref[i] == 0)
    def _():
        o_ref[...] = jnp.zeros_like(o_ref)


def _dispatch(tok_of_slot, sub_live, h_slabs, d, tr):
    rows = tok_of_slot.shape[0]
    slabs = d // (2 * LANES)
    grid_spec = pltpu.PrefetchScalarGridSpec(
        num_scalar_prefetch=2,
        grid=(rows // tr,),
        in_specs=[pl.BlockSpec(memory_space=pl.ANY)],
        out_specs=pl.BlockSpec((tr, d), lambda i, tok, live: (i, 0)),
        scratch_shapes=[pltpu.VMEM((GATHER_DEPTH, tr * _pitch(slabs), LANES), jnp.uint32),
                        pltpu.SemaphoreType.DMA((GATHER_DEPTH,))],
    )
    return pl.pallas_call(
        functools.partial(_dispatch_kernel, slabs=slabs),
        grid_spec=grid_spec,
        out_shape=jax.ShapeDtypeStruct((rows, d), BF16),
        compiler_params=_params(("arbitrary",)),
        name="moe_dispatch",
    )(tok_of_slot, sub_live, h_slabs)


def _expert_ffn_kernel(pe_ref, ps_ref, pn_ref, pt_ref, used_ref, xs_ref, wg_ref, wu_ref, wd_ref, y_ref,
                       xv_ref, acc_ref, stage_ref, sem, *, tr, slabs):
    p = pl.program_id(0)
    c = pl.program_id(1)
    last = pl.num_programs(1) - 1
    nsub = pn_ref[p]
    sub0 = ps_ref[p]
    live = nsub > 0

    def rows_of(r, rows=tr):
        return pl.ds(pl.multiple_of(r * tr, tr), rows)

    def x_copy(r):
        return pltpu.make_async_copy(
            xs_ref.at[pl.ds(pl.multiple_of((sub0 + r) * tr, tr), tr), :],
            xv_ref.at[rows_of(r), :], sem.at[0])

    def x_start(r, carry):
        x_copy(r).start()
        return carry

    def x_wait(r, carry):
        x_copy(r).wait()
        return carry

    @pl.when(jnp.logical_and(live, c == 0))
    def _():
        lax.fori_loop(0, nsub, x_start, 0)
        lax.fori_loop(0, nsub, x_wait, 0)

    def hidden(r, rows):
        x = xv_ref[rows_of(r, rows), :]
        g = jnp.dot(x, wg_ref[...].astype(BF16), preferred_element_type=F32)
        u = jnp.dot(x, wu_ref[...].astype(BF16), preferred_element_type=F32)
        return (g * jax.nn.sigmoid(g) * u).astype(BF16)

    def y_copy(sub, slot):
        return pltpu.make_async_copy(
            stage_ref.at[slot],
            y_ref.at[pl.ds(pl.multiple_of(sub * tr * slabs, tr * slabs), tr * slabs), :],
            sem.at[1 + slot])

    def first_apply(r, slot, part):
        acc_ref[rows_of(r, part.shape[0]), :] = part

    def middle_apply(r, slot, part):
        acc_ref[rows_of(r, part.shape[0]), :] += part

    def final_apply(r, slot, part):
        rows = part.shape[0]
        res = acc_ref[rows_of(r, rows), :] + part

        @pl.when(r >= 2)
        def _():
            y_copy(sub0 + r - 2, slot).wait()

        for f in range(slabs):
            stage_ref[slot, pl.ds(f, rows, stride=slabs), :] = _pack_pairs(res, f)
        if rows < tr:
            stage_ref[slot, rows * slabs:, :] = jnp.zeros(((tr - rows) * slabs, LANES), jnp.uint32)
        y_copy(sub0 + r, slot).start()

    def for_each_subtile(apply, group):
        def run(first, count, rows=tr):
            hids = [hidden(first + t, rows) for t in range(count)]
            for t in range(count):
                part = jnp.dot(hids[t], wd_ref[...].astype(BF16), preferred_element_type=F32)
                apply(first + t, t % 2, part)

        def body(q, carry):
            run(group * q, group)
            return carry

        lax.fori_loop(0, nsub // group, body, 0)
        rest = nsub % group
        size = group // 2
        while size > 1:
            @pl.when((rest // size) % 2 == 1)
            def _(size=size):
                run(nsub - rest % (2 * size), size)
            size //= 2
        short = pt_ref[p] <= tr // 2

        @pl.when(jnp.logical_and(rest % 2 == 1, short))
        def _():
            run(nsub - 1, 1, tr // 2)

        @pl.when(jnp.logical_and(rest % 2 == 1, jnp.logical_not(short)))
        def _():
            run(nsub - 1, 1)

    @pl.when(live)
    def _():
        @pl.when(c == 0)
        def _():
            for_each_subtile(first_apply, 2)

        @pl.when(jnp.logical_and(c > 0, c < last))
        def _():
            for_each_subtile(middle_apply, 4)

        @pl.when(c == last)
        def _():
            for_each_subtile(final_apply, 2)

            @pl.when(nsub >= 2)
            def _():
                y_copy(sub0 + nsub - 2, nsub % 2).wait()

            y_copy(sub0 + nsub - 1, (nsub - 1) % 2).wait()

    @pl.when(jnp.logical_and(p == pl.num_programs(0) - 1, c == last))
    def _():
        stage_ref[0] = jnp.zeros(stage_ref.shape[1:], jnp.uint32)

        def zero(s, carry):
            y_copy(s, 0).start()
            y_copy(s, 0).wait()
            return carry

        lax.fori_loop(used_ref[0], y_ref.shape[0] // (tr * slabs), zero, 0)


def _expert_ffn(xs, pass_table, used, w_gu, w_down, layer, tr, cap_sub, tc):
    rows, d = xs.shape
    fe = w_down.shape[2]
    nc = fe // tc
    assert nc >= 2
    slabs = d // (2 * LANES)
    last = nc - 1

    def chunk(c, pn, p):
        live = jnp.minimum(pn[p], 1)
        return c * live + last * (1 - live)

    grid_spec = pltpu.PrefetchScalarGridSpec(
        num_scalar_prefetch=5,
        grid=(pass_table[0].shape[0], nc),
        in_specs=[
            pl.BlockSpec(memory_space=pl.ANY),
            pl.BlockSpec((None, None, d, tc),
                         lambda p, c, pe, ps, pn, pt, us: (layer, pe[p], 0, chunk(c, pn, p))),
            pl.BlockSpec((None, None, d, tc),
                         lambda p, c, pe, ps, pn, pt, us: (layer, pe[p], 0, nc + chunk(c, pn, p))),
            pl.BlockSpec((None, None, tc, d),
                         lambda p, c, pe, ps, pn, pt, us: (layer, pe[p], chunk(c, pn, p), 0)),
        ],
        out_specs=pl.BlockSpec(memory_space=pl.ANY),
        scratch_shapes=[pltpu.VMEM((cap_sub * tr, d), BF16), pltpu.VMEM((cap_sub * tr, d), F32),
                        pltpu.VMEM((2, tr * slabs, LANES), jnp.uint32),
                        pltpu.SemaphoreType.DMA((3,))],
    )
    return pl.pallas_call(
        functools.partial(_expert_ffn_kernel, tr=tr, slabs=slabs),
        grid_spec=grid_spec,
        out_shape=jax.ShapeDtypeStruct((rows * slabs, LANES), jnp.uint32),
        compiler_params=_params(("arbitrary", "arbitrary")),
        name="moe_expert_swiglu",
    )(*pass_table, used, xs, w_gu, w_gu, w_down)


def _combine_kernel(s1_ref, s2_ref, y_ref, rw_ref, x_ref, g_ref, o_ref, b1_ref, b2_ref, sem,
                    *, slabs):
    i = pl.program_id(0)
    last = pl.num_programs(0) - 1
    tm = x_ref.shape[0]
    slot = i % GATHER_DEPTH

    def fetch(tile, wait):
        s = tile % GATHER_DEPTH
        _gather_rows(y_ref, b1_ref.at[s], sem.at[0, s], s1_ref, tile * tm, tm, slabs, wait)
        _gather_rows(y_ref, b2_ref.at[s], sem.at[1, s], s2_ref, tile * tm, tm, slabs, wait)

    @pl.when(i == 0)
    def _():
        for t in range(GATHER_DEPTH - 1):
            @pl.when(t <= last)
            def _(t=t):
                fetch(t, False)

    @pl.when(i + GATHER_DEPTH - 1 <= last)
    def _():
        fetch(i + GATHER_DEPTH - 1, False)

    fetch(i, True)

    rw = rw_ref[...]
    w1 = rw[:, 0:1]
    w2 = rw[:, 1:2]
    parts = [None] * (2 * slabs)
    ssq = jnp.zeros((tm, 1), F32)
    for f in range(slabs):
        lo1, hi1 = _unpack_pairs(b1_ref[slot, pl.ds(f, tm, stride=_pitch(slabs)), :])
        lo2, hi2 = _unpack_pairs(b2_ref[slot, pl.ds(f, tm, stride=_pitch(slabs)), :])
        for blk, a, b in ((f, lo1, lo2), (f + slabs, hi1, hi2)):
            m = w1 * a + w2 * b
            ssq = ssq + jnp.sum(m * m, axis=1, keepdims=True)
            parts[blk] = m
    inv = lax.rsqrt(ssq / (2 * slabs * LANES) + NORM_EPS)
    for blk in range(2 * slabs):
        sl = slice(blk * LANES, (blk + 1) * LANES)
        o_ref[:, sl] = x_ref[:, sl] + parts[blk] * inv * g_ref[:, sl]


def _combine(slot1, slot2, y_slabs, rw, x, g, tm):
    n, d = x.shape
    slabs = d // (2 * LANES)
    grid_spec = pltpu.PrefetchScalarGridSpec(
        num_scalar_prefetch=2,
        grid=(n // tm,),
        in_specs=[
            pl.BlockSpec(memory_space=pl.ANY),
            pl.BlockSpec((tm, LANES), lambda i, s1, s2: (i, 0)),
            pl.BlockSpec((tm, d), lambda i, s1, s2: (i, 0)),
            pl.BlockSpec((1, d), lambda i, s1, s2: (0, 0)),
        ],
        out_specs=pl.BlockSpec((tm, d), lambda i, s1, s2: (i, 0)),
        scratch_shapes=[pltpu.VMEM((GATHER_DEPTH, tm * _pitch(slabs), LANES), jnp.uint32),
                        pltpu.VMEM((GATHER_DEPTH, tm * _pitch(slabs), LANES), jnp.uint32),
                        pltpu.SemaphoreType.DMA((2, GATHER_DEPTH))],
    )
    return pl.pallas_call(
        functools.partial(_combine_kernel, slabs=slabs),
        grid_spec=grid_spec,
        out_shape=jax.ShapeDtypeStruct((n, d), F32),
        compiler_params=_params(("arbitrary",)),
        name="moe_combine",
    )(slot1, slot2, y_slabs, rw, x, g)


def _moe_ffn(x, g_in, g_out, w_router, w_gu, w_down, layer, tiles):
    n, d = x.shape
    experts = w_router.shape[1]
    assert d % (2 * LANES * SUBLANES) == 0
    tr = tiles["expert_rows"]
    cap_sub = tiles["expert_cap"] // tr
    h_slabs, ri, rw, cnt = _router(x, g_in, w_router, tiles["router_rows"])

    counts = cnt[0, :experts]
    nsub = (counts + tr - 1) // tr
    sub_end = jnp.cumsum(nsub)
    sub_start = sub_end - nsub
    slot1 = (sub_start[ri[:, 0]] * tr + ri[:, 2]).astype(jnp.int32)
    slot2 = (sub_start[ri[:, 1]] * tr + ri[:, 3]).astype(jnp.int32)
    rows = TOP_K * n + experts * tr
    n_sub = rows // tr
    token = jnp.arange(n, dtype=jnp.int32)
    tok_of_slot = jnp.zeros((rows,), jnp.int32).at[jnp.concatenate([slot1, slot2])].set(
        jnp.concatenate([token, token]), unique_indices=True)
    sub_live = (jnp.arange(n_sub, dtype=jnp.int32) < sub_end[-1]).astype(jnp.int32)

    n_pass = experts + n_sub // cap_sub
    passes = (nsub + cap_sub - 1) // cap_sub
    pass_end = jnp.cumsum(passes)
    pidx = jnp.arange(n_pass, dtype=jnp.int32)
    pe = jnp.minimum(jnp.searchsorted(pass_end, pidx, side="right"), experts - 1)
    local = pidx - (pass_end[pe] - passes[pe])
    p_live = pidx < pass_end[-1]
    pass_nsub = jnp.where(p_live, jnp.minimum(cap_sub, nsub[pe] - local * cap_sub), 0)
    pass_sub0 = jnp.where(p_live, sub_start[pe] + local * cap_sub, 0)
    last_pe = jnp.max(jnp.where(passes > 0, jnp.arange(experts), 0))
    pass_expert = jnp.where(p_live, pe, last_pe)
    pass_tokens = jnp.minimum(cap_sub * tr, counts[pe] - local * cap_sub * tr)
    pass_tail = jnp.where(p_live, pass_tokens - (pass_nsub - 1) * tr, 0)
    pass_table = tuple(a.astype(jnp.int32) for a in (pass_expert, pass_sub0, pass_nsub, pass_tail))

    xs = _dispatch(tok_of_slot, sub_live, h_slabs, d, tr)
    y_slabs = _expert_ffn(xs, pass_table, sub_end[-1:].astype(jnp.int32), w_gu, w_down, layer, tr,
                          cap_sub, tiles["expert_chunk"])
    return _combine(slot1, slot2, y_slabs, rw, x, g_out, tiles["combine_rows"])


def _tiles(n, seq, d, ffn, expert_ffn):
    pick = lambda pref, dim: min(pref, dim)
    return {
        "conv_rows": pick(256, seq), "conv_cols": pick(512, d),
        "ln_rows": pick(256, n),
        "ffn_rows": pick(512, n), "ffn_chunk": pick(512, ffn),
        "proj_rows": pick(256, seq),
        "attn_q": pick(512, seq), "attn_k": pick(512, seq),
        "out_rows": pick(256, n),
        "router_rows": pick(512, n),
        "expert_rows": pick(256, n), "expert_cap": pick(2304, TOP_K * n),
        "expert_chunk": pick(256, expert_ffn),
        "combine_rows": pick(256, n),
    }


def _rope_tables(seq):
    pos = jnp.arange(seq, dtype=F32)
    inv_freq = ROPE_THETA ** (-jnp.arange(0, QK_ROPE_DIM, 2, dtype=F32) / QK_ROPE_DIM)
    ang = pos[:, None] * inv_freq[None, :]
    cos, sin = jnp.cos(ang), jnp.sin(ang)
    zeros = jnp.zeros((seq, LANES - QK_ROPE_DIM), F32)
    return (jnp.concatenate([cos, cos, zeros], axis=1),
            jnp.concatenate([-sin, sin, zeros], axis=1))


def _swap_halves(w):
    half = w.shape[-1] // 2
    return jnp.concatenate([w[..., half:], w[..., :half]], axis=-1)


def kernel(x, norm_g, conv_w_pw1, conv_b_pw1, conv_w_dw, conv_b_dw, conv_ln_g, conv_ln_b,
           conv_w_pw2, ffn_w_gu, ffn_w_down, moe_w_router, moe_w_gu, moe_w_down,
           kv_in_g, kv_w_a, kv_latent_g, kv_w_b, attn_w_q_a, attn_q_latent_g,
           attn_w_q_b, attn_w_o):
    batch, seq, d = x.shape
    n = batch * seq
    depth = norm_g.shape[0]
    n_conv = depth // 2
    heads = d // 128
    kv_rank = kv_latent_g.shape[0]
    tiles = _tiles(n, seq, d, ffn_w_down.shape[1], moe_w_down.shape[2])
    cos2, sin2 = _rope_tables(seq)
    vec = lambda v: v.reshape(1, -1)

    xs = x.reshape(n, d)
    k = v = None
    for i in range(depth):
        if i < n_conv:
            c = _pw1_dwconv(xs, vec(norm_g[i, 0]), conv_w_pw1[i].astype(BF16), vec(conv_b_pw1[i]),
                            conv_w_dw[i], vec(conv_b_dw[i]), seq, tiles["conv_rows"],
                            tiles["conv_cols"])
            xs = _ln_pw2(c, xs, vec(conv_ln_g[i]), vec(conv_ln_b[i]), conv_w_pw2[i].astype(BF16),
                         vec(norm_g[i, 1]), tiles["ln_rows"])
        else:
            j = i - n_conv
            q_rank = attn_w_q_a.shape[2]
            wqb = attn_w_q_b[j].reshape(q_rank, heads, QK_NOPE_DIM + QK_ROPE_DIM)
            rope_w = wqb[:, :, QK_NOPE_DIM:]
            wqb_ext = jnp.concatenate([wqb, _swap_halves(rope_w)], axis=2)
            wqb_ext = wqb_ext.reshape(q_rank, heads * 256).astype(BF16)
            q_args = (attn_w_q_a[j].astype(BF16), vec(attn_q_latent_g[j]), wqb_ext, cos2, sin2,
                      seq, tiles["proj_rows"])
            if j == 0:
                rope_w = kv_w_a[:, kv_rank:]
                wa_ext = jnp.concatenate([kv_w_a, _swap_halves(rope_w)], axis=1).astype(BF16)
                wb = kv_w_b.reshape(kv_rank, heads, QK_NOPE_DIM + V_DIM)
                wb_split = jnp.concatenate(
                    [wb[:, :, :QK_NOPE_DIM].reshape(kv_rank, heads * QK_NOPE_DIM),
                     wb[:, :, QK_NOPE_DIM:].reshape(kv_rank, heads * V_DIM)], axis=1).astype(BF16)
                k, v, q = _qkv_proj(xs, vec(kv_in_g), vec(norm_g[i, 0]), wa_ext, vec(kv_latent_g),
                                    wb_split, *q_args)
            else:
                q = _q_proj(xs, vec(norm_g[i, 0]), *q_args)
            o = _attention(q, k, v, batch, seq, tiles["attn_q"], tiles["attn_k"])
            xs = _attn_out(o, xs, attn_w_o[j].astype(BF16), vec(norm_g[i, 1]), tiles["out_rows"])
        if i % 2 == 0:
            xs = _dense_ffn(xs, vec(norm_g[i, 2]), vec(norm_g[i, 3]), ffn_w_gu[i // 2].astype(BF16),
                            ffn_w_down[i // 2].astype(BF16), tiles["ffn_rows"], tiles["ffn_chunk"])
        else:
            xs = _moe_ffn(xs, vec(norm_g[i, 2]), vec(norm_g[i, 3]), moe_w_router[i // 2],
                          moe_w_gu, moe_w_down, i // 2, tiles)
    return xs.reshape(batch, seq, d)
```

```python
import functools

import jax
import jax.numpy as jnp
from jax import lax
from jax.experimental import pallas as pl
from jax.experimental.pallas import tpu as pltpu

NORM_EPS = 1e-6
ROPE_THETA = 10000.0
QK_NOPE_DIM = 128
QK_ROPE_DIM = 64
V_DIM = 128
TOP_K = 2
_Q_SCALE = float(QK_NOPE_DIM + QK_ROPE_DIM) ** -0.5 * 1.4426950408889634

LANES = 128
SUBLANES = 8
CONV_HALO = 32
GATHER_UNROLL = 8
GATHER_DEPTH = 3
VMEM_LIMIT = 56 * 1024 * 1024

F32 = jnp.float32
BF16 = jnp.bfloat16


def _params(semantics):
    return pltpu.CompilerParams(dimension_semantics=semantics, vmem_limit_bytes=VMEM_LIMIT)


def _rms(x):
    return x * lax.rsqrt(jnp.mean(x * x, axis=-1, keepdims=True) + NORM_EPS)


_HIGH_HALF = 0xFFFF0000


def _pack_pairs(x, f):
    half = x.shape[1] // (2 * LANES)
    lo = x[:, f * LANES:(f + 1) * LANES].astype(BF16).astype(F32)
    hi = x[:, (f + half) * LANES:(f + half + 1) * LANES].astype(BF16).astype(F32)
    lo_bits = lax.shift_right_logical(lax.bitcast_convert_type(lo, jnp.uint32), jnp.uint32(16))
    hi_bits = lax.bitcast_convert_type(hi, jnp.uint32) & jnp.uint32(_HIGH_HALF)
    return hi_bits | lo_bits


def _unpack_pairs(w):
    lo = lax.bitcast_convert_type(lax.shift_left(w, jnp.uint32(16)), F32)
    hi = lax.bitcast_convert_type(w & jnp.uint32(_HIGH_HALF), F32)
    return lo, hi


def _pw1_dwconv_kernel(x_ref, g_ref, w_ref, b_ref, wdw_ref, bdw_ref, o_ref,
                       xn_ref, carry_ref, sh_ref, *, tiles_per_seq, tn, rb, cb):
    tm, d = x_ref.shape
    width = wdw_ref.shape[0]
    ext = tm + CONV_HALO
    base = CONV_HALO - (width - 1)
    first = (pl.program_id(0) % tiles_per_seq) == 0
    xn_ref[...] = (_rms(x_ref[...]) * g_ref[...]).astype(BF16)

    @pl.when(pl.program_id(0) == 0)
    def _():
        carry_ref[...] = jnp.zeros_like(carry_ref)

    for j in range(d // tn):
        cols = slice(j * tn, (j + 1) * tn)
        gate_cols = slice(d + j * tn, d + (j + 1) * tn)
        a = jnp.dot(xn_ref[...], w_ref[:, cols], preferred_element_type=F32) + b_ref[:, cols]
        g = jnp.dot(xn_ref[...], w_ref[:, gate_cols], preferred_element_type=F32) + b_ref[:, gate_cols]
        u = a * jax.nn.sigmoid(g)
        sh = sh_ref.at[j % 2]
        halo = carry_ref[:, cols]
        sh[0, 0:CONV_HALO, :] = jnp.where(first, jnp.zeros_like(halo), halo)
        sh[0, CONV_HALO:ext, :] = u
        carry_ref[:, cols] = u[tm - CONV_HALO:, :]
        for s in range(1, SUBLANES):
            sh[s, 0:ext - SUBLANES, :] = sh[0, s:ext - SUBLANES + s, :]
        for r0 in range(0, tm, rb):
            for c0 in range(0, tn, cb):
                acc = None
                for k in range(width):
                    s = (base + k) % SUBLANES
                    off = (base + k) - s
                    term = (wdw_ref[k:k + 1, j * tn + c0:j * tn + c0 + cb]
                            * sh[s, r0 + off:r0 + off + rb, c0:c0 + cb])
                    acc = term if acc is None else acc + term
                o_ref[r0:r0 + rb, j * tn + c0:j * tn + c0 + cb] = (
                    acc + bdw_ref[:, j * tn + c0:j * tn + c0 + cb]).astype(o_ref.dtype)


def _pw1_dwconv(x, g, w, b, w_dw, b_dw, seq, tm, tn):
    n, d = x.shape
    width = w_dw.shape[0]
    assert width - 1 <= CONV_HALO <= tm
    kern = functools.partial(_pw1_dwconv_kernel, tiles_per_seq=seq // tm, tn=tn,
                             rb=min(64, tm), cb=min(256, tn))
    const = lambda shape: pl.BlockSpec(shape, lambda i: (0, 0), pipeline_mode=pl.Buffered(1))
    return pl.pallas_call(
        kern,
        grid=(n // tm,),
        in_specs=[pl.BlockSpec((tm, d), lambda i: (i, 0)), const((1, d)), const(w.shape),
                  const(b.shape), const((width, d)), const((1, d))],
        out_specs=pl.BlockSpec((tm, d), lambda i: (i, 0)),
        out_shape=jax.ShapeDtypeStruct((n, d), BF16),
        scratch_shapes=[pltpu.VMEM((tm, d), BF16), pltpu.VMEM((CONV_HALO, d), F32),
                        pltpu.VMEM((2, SUBLANES, tm + CONV_HALO, tn), F32)],
        compiler_params=_params(("arbitrary",)),
        name="conv_pw1_dwconv",
    )(x, g, w, b, w_dw, b_dw)


def _ln_pw2_kernel(c_ref, x_ref, lg_ref, lb_ref, w_ref, g_ref, o_ref):
    c = c_ref[...].astype(F32)
    mu = jnp.mean(c, axis=-1, keepdims=True)
    cc = c - mu
    y = cc * lax.rsqrt(jnp.mean(cc * cc, axis=-1, keepdims=True) + NORM_EPS)
    y = y * lg_ref[...] + lb_ref[...]
    y = (y * jax.nn.sigmoid(y)).astype(BF16)
    m = jnp.dot(y, w_ref[...], preferred_element_type=F32)
    o_ref[...] = x_ref[...] + _rms(m) * g_ref[...]


def _ln_pw2(c, x, ln_g, ln_b, w, g, tm):
    n, d = x.shape
    row = pl.BlockSpec((tm, d), lambda i: (i, 0))
    vec = pl.BlockSpec((1, d), lambda i: (0, 0))
    return pl.pallas_call(
        _ln_pw2_kernel,
        grid=(n // tm,),
        in_specs=[row, row, vec, vec, pl.BlockSpec((d, d), lambda i: (0, 0)), vec],
        out_specs=row,
        out_shape=jax.ShapeDtypeStruct((n, d), F32),
        compiler_params=_params(("parallel",)),
        name="conv_ln_pw2",
    )(c, x, ln_g, ln_b, w, g)


def _swiglu_chunk(xn_ref, wg_ref, wu_ref, wd_ref, acc_ref, groups):
    rows = xn_ref.shape[0] // groups
    hids = []
    for i in range(groups):
        xn = xn_ref[i * rows:(i + 1) * rows, :]
        g = jnp.dot(xn, wg_ref[...], preferred_element_type=F32)
        u = jnp.dot(xn, wu_ref[...], preferred_element_type=F32)
        hids.append((g * jax.nn.sigmoid(g) * u).astype(BF16))
    for i in range(groups):
        acc_ref[i * rows:(i + 1) * rows, :] += jnp.dot(hids[i], wd_ref[...],
                                                       preferred_element_type=F32)


def _dense_ffn_kernel(x_ref, gi_ref, go_ref, wg_ref, wu_ref, wd_ref, o_ref, xn_ref, acc_ref):
    c = pl.program_id(1)

    @pl.when(c == 0)
    def _():
        xn_ref[...] = (_rms(x_ref[...]) * gi_ref[...]).astype(BF16)
        acc_ref[...] = jnp.zeros_like(acc_ref)

    _swiglu_chunk(xn_ref, wg_ref, wu_ref, wd_ref, acc_ref, groups=2)

    @pl.when(c == pl.num_programs(1) - 1)
    def _():
        o_ref[...] = x_ref[...] + _rms(acc_ref[...]) * go_ref[...]


def _dense_ffn(x, g_in, g_out, w_gu, w_down, tm, tc):
    n, d = x.shape
    f = w_down.shape[0]
    nc = f // tc
    row = pl.BlockSpec((tm, d), lambda i, c: (i, 0))
    vec = pl.BlockSpec((1, d), lambda i, c: (0, 0))
    return pl.pallas_call(
        _dense_ffn_kernel,
        grid=(n // tm, nc),
        in_specs=[
            row, vec, vec,
            pl.BlockSpec((d, tc), lambda i, c: (0, c)),
            pl.BlockSpec((d, tc), lambda i, c: (0, nc + c)),
            pl.BlockSpec((tc, d), lambda i, c: (c, 0)),
        ],
        out_specs=row,
        out_shape=jax.ShapeDtypeStruct((n, d), F32),
        scratch_shapes=[pltpu.VMEM((tm, d), BF16), pltpu.VMEM((tm, d), F32)],
        compiler_params=_params(("parallel", "arbitrary")),
        name="dense_swiglu",
    )(x, g_in, g_out, w_gu, w_gu, w_down)


def _rope128(t, cos2, sin2):
    return t * cos2 + pltpu.roll(t, 64, axis=1) * sin2


def _q_proj_kernel(x_ref, g_ref, wa_ref, lg_ref, wb_ref, cos_ref, sin_ref, q_ref, *, heads, scale):
    h = (_rms(x_ref[...]) * g_ref[...]).astype(BF16)
    lat = jnp.dot(h, wa_ref[...], preferred_element_type=F32)
    lat = (_rms(lat) * lg_ref[...]).astype(BF16)
    q = jnp.dot(lat, wb_ref[...], preferred_element_type=F32)
    cos2 = cos_ref[...]
    sin2 = sin_ref[...]
    for hh in range(heads):
        q_ref[:, hh * 256:hh * 256 + 128] = (q[:, hh * 256:hh * 256 + 128] * scale).astype(BF16)
        r = _rope128(q[:, hh * 256 + 128:(hh + 1) * 256], cos2, sin2)
        q_ref[:, hh * 256 + 128:(hh + 1) * 256] = (r * scale).astype(BF16)


def _q_proj(x, g, wa, lat_g, wb_ext, cos2, sin2, seq, tm):
    n, d = x.shape
    heads = d // 128
    spt = seq // tm
    kern = functools.partial(_q_proj_kernel, heads=heads, scale=_Q_SCALE)
    full = lambda a: pl.BlockSpec(a.shape, lambda i: (0, 0))
    return pl.pallas_call(
        kern,
        grid=(n // tm,),
        in_specs=[
            pl.BlockSpec((tm, d), lambda i: (i, 0)), full(g), full(wa), full(lat_g), full(wb_ext),
            pl.BlockSpec((tm, LANES), lambda i: (i % spt, 0)),
            pl.BlockSpec((tm, LANES), lambda i: (i % spt, 0)),
        ],
        out_specs=pl.BlockSpec((tm, heads * 256), lambda i: (i, 0)),
        out_shape=jax.ShapeDtypeStruct((n, heads * 256), BF16),
        compiler_params=_params(("parallel",)),
        name="mla_q_proj",
    )(x, g, wa, lat_g, wb_ext, cos2, sin2)


def _qkv_proj_kernel(x_ref, gkv_ref, gq_ref, wkva_ref, lkv_ref, wkvb_ref, wqa_ref, lq_ref, wqb_ref,
                     cos_ref, sin_ref, k_ref, v_ref, q_ref, *, heads, rank, scale):
    xhat = _rms(x_ref[...])
    h_kv = (xhat * gkv_ref[...]).astype(BF16)
    h_q = (xhat * gq_ref[...]).astype(BF16)
    a = jnp.dot(h_kv, wkva_ref[...], preferred_element_type=F32)
    lat = jnp.dot(h_q, wqa_ref[...], preferred_element_type=F32)
    c_kv = (_rms(a[:, :rank]) * lkv_ref[...]).astype(BF16)
    lat = (_rms(lat) * lq_ref[...]).astype(BF16)
    cos2 = cos_ref[...]
    sin2 = sin_ref[...]
    k_rope = _rope128(a[:, rank:], cos2, sin2).astype(BF16)
    kv = jnp.dot(c_kv, wkvb_ref[...], preferred_element_type=F32)
    q = jnp.dot(lat, wqb_ref[...], preferred_element_type=F32)
    hd = heads * QK_NOPE_DIM
    for hh in range(heads):
        k_ref[:, hh * 256:hh * 256 + 128] = kv[:, hh * 128:(hh + 1) * 128].astype(BF16)
        k_ref[:, hh * 256 + 128:(hh + 1) * 256] = k_rope
    v_ref[...] = kv[:, hd:].astype(BF16)
    for hh in range(heads):
        q_ref[:, hh * 256:hh * 256 + 128] = (q[:, hh * 256:hh * 256 + 128] * scale).astype(BF16)
        r = _rope128(q[:, hh * 256 + 128:(hh + 1) * 256], cos2, sin2)
        q_ref[:, hh * 256 + 128:(hh + 1) * 256] = (r * scale).astype(BF16)


def _qkv_proj(x, g_kv, g_q, wkva_ext, lat_kv, wkvb_split, wqa, lat_q, wqb_ext, cos2, sin2, seq, tm):
    n, d = x.shape
    heads = d // 128
    spt = seq // tm
    kern = functools.partial(_qkv_proj_kernel, heads=heads, rank=lat_kv.shape[1], scale=_Q_SCALE)
    full = lambda a: pl.BlockSpec(a.shape, lambda i: (0, 0))
    rope = pl.BlockSpec((tm, LANES), lambda i: (i % spt, 0))
    wide = pl.BlockSpec((tm, heads * 256), lambda i: (i, 0))
    return pl.pallas_call(
        kern,
        grid=(n // tm,),
        in_specs=[pl.BlockSpec((tm, d), lambda i: (i, 0)), full(g_kv), full(g_q), full(wkva_ext),
                  full(lat_kv), full(wkvb_split), full(wqa), full(lat_q), full(wqb_ext), rope, rope],
        out_specs=[wide, pl.BlockSpec((tm, heads * 128), lambda i: (i, 0)), wide],
        out_shape=[jax.ShapeDtypeStruct((n, heads * 256), BF16),
                   jax.ShapeDtypeStruct((n, heads * 128), BF16),
                   jax.ShapeDtypeStruct((n, heads * 256), BF16)],
        compiler_params=_params(("parallel",)),
        name="mla_qkv_proj",
    )(x, g_kv, g_q, wkva_ext, lat_kv, wkvb_split, wqa, lat_q, wqb_ext, cos2, sin2)


def _attn_kernel(q_ref, k_ref, v_ref, o_ref, *, tq, tk):
    seq = q_ref.shape[0]
    row = lax.broadcasted_iota(jnp.int32, (tq, tk), 0)
    col = lax.broadcasted_iota(jnp.int32, (tq, tk), 1)
    for qi in range(seq // tq):
        q = q_ref[qi * tq:(qi + 1) * tq, :]
        m = jnp.full((tq, 1), -jnp.inf, F32)
        l = jnp.zeros((tq, 1), F32)
        acc = jnp.zeros((tq, V_DIM), F32)
        for j in range((qi + 1) * tq // tk):
            k = k_ref[j * tk:(j + 1) * tk, :]
            s = lax.dot_general(q, k, (((1,), (1,)), ((), ())), preferred_element_type=F32)
            if (j + 1) * tk > qi * tq + 1:
                s = jnp.where(row + qi * tq >= col + j * tk, s, -jnp.inf)
            m_new = jnp.maximum(m, jnp.max(s, axis=1, keepdims=True))
            p = jnp.exp2(s - m_new)
            alpha = jnp.exp2(m - m_new)
            l = alpha * l + jnp.sum(p, axis=1, keepdims=True)
            pv = jnp.dot(p.astype(BF16), v_ref[j * tk:(j + 1) * tk, :],
                         preferred_element_type=F32)
            acc = alpha * acc + pv
            m = m_new
        o_ref[qi * tq:(qi + 1) * tq, :] = (acc / l).astype(o_ref.dtype)


def _attention(q, k, v, batch, seq, tq, tk):
    n = q.shape[0]
    heads = v.shape[1] // V_DIM
    kern = functools.partial(_attn_kernel, tq=tq, tk=tk)
    return pl.pallas_call(
        kern,
        grid=(batch, heads),
        in_specs=[
            pl.BlockSpec((seq, 256), lambda b, h: (b, h)),
            pl.BlockSpec((seq, 256), lambda b, h: (b, h)),
            pl.BlockSpec((seq, V_DIM), lambda b, h: (b, h)),
        ],
        out_specs=pl.BlockSpec((seq, V_DIM), lambda b, h: (b, h)),
        out_shape=jax.ShapeDtypeStruct((n, heads * V_DIM), BF16),
        compiler_params=_params(("parallel", "parallel")),
        name="mla_attention",
    )(q, k, v)


def _attn_out_kernel(o_ref, x_ref, w_ref, g_ref, xo_ref):
    m = jnp.dot(o_ref[...], w_ref[...], preferred_element_type=F32)
    xo_ref[...] = x_ref[...] + _rms(m) * g_ref[...]


def _attn_out(o, x, w, g, tm):
    n, d = x.shape
    row = pl.BlockSpec((tm, d), lambda i: (i, 0))
    return pl.pallas_call(
        _attn_out_kernel,
        grid=(n // tm,),
        in_specs=[pl.BlockSpec((tm, o.shape[1]), lambda i: (i, 0)), row,
                  pl.BlockSpec(w.shape, lambda i: (0, 0)), pl.BlockSpec((1, d), lambda i: (0, 0))],
        out_specs=row,
        out_shape=jax.ShapeDtypeStruct((n, d), F32),
        compiler_params=_params(("parallel",)),
        name="mla_out_proj",
    )(o, x, w, g)


def _router_kernel(x_ref, g_ref, wcat_ref, h_ref, ri_ref, rw_ref, cnt_ref, base_ref,
                   *, experts, slabs):
    tm = x_ref.shape[0]

    @pl.when(pl.program_id(0) == 0)
    def _():
        base_ref[...] = jnp.zeros_like(base_ref)

    h = _rms(x_ref[...]) * g_ref[...]
    for f in range(slabs):
        h_ref[pl.ds(f, tm, stride=slabs), :] = _pack_pairs(h, f)

    h_hi = h.astype(BF16)
    h_lo = (h - h_hi.astype(F32)).astype(BF16)
    hi = jnp.dot(h_hi, wcat_ref[...], preferred_element_type=F32)
    logits = (hi[:, :LANES] + hi[:, LANES:]
              + jnp.dot(h_lo, wcat_ref[:, :LANES], preferred_element_type=F32))

    lane = lax.broadcasted_iota(jnp.int32, (tm, LANES), 1)
    lg = jnp.where(lane < experts, logits, -jnp.inf)
    m1 = jnp.max(lg, axis=1, keepdims=True)
    i1 = jnp.min(jnp.where(lg == m1, lane, LANES), axis=1, keepdims=True)
    oh1 = lane == i1
    lg2 = jnp.where(oh1, -jnp.inf, lg)
    m2 = jnp.max(lg2, axis=1, keepdims=True)
    i2 = jnp.min(jnp.where(lg2 == m2, lane, LANES), axis=1, keepdims=True)
    oh2 = lane == i2
    e2 = jnp.exp(m2 - m1)
    w1 = 1.0 / (1.0 + e2)
    w2 = e2 / (1.0 + e2)

    cnt = oh1.astype(F32) + oh2.astype(F32)
    r_i = lax.broadcasted_iota(jnp.int32, (tm, tm), 0)
    c_i = lax.broadcasted_iota(jnp.int32, (tm, tm), 1)
    tri = (c_i < r_i).astype(BF16)
    before = jnp.dot(tri, cnt.astype(BF16), preferred_element_type=F32) + base_ref[0:1, :]
    rank1 = jnp.sum(jnp.where(oh1, before, 0.0), axis=1, keepdims=True).astype(jnp.int32)
    rank2 = jnp.sum(jnp.where(oh2, before, 0.0), axis=1, keepdims=True).astype(jnp.int32)
    total = base_ref[0:1, :] + jnp.sum(cnt, axis=0, keepdims=True)
    base_ref[...] = jnp.broadcast_to(total, base_ref.shape)
    cnt_ref[...] = jnp.broadcast_to(total, cnt_ref.shape).astype(jnp.int32)

    ri_ref[...] = jnp.where(lane == 0, i1, jnp.where(lane == 1, i2,
                            jnp.where(lane == 2, rank1, jnp.where(lane == 3, rank2, 0))))
    rw_ref[...] = jnp.where(lane == 0, w1, jnp.where(lane == 1, w2, 0.0))


def _router(x, g, w_router, tm):
    n, d = x.shape
    experts = w_router.shape[1]
    slabs = d // (2 * LANES)
    w_pad = jnp.zeros((d, LANES), F32).at[:, :experts].set(w_router)
    w_hi = w_pad.astype(BF16)
    w_lo = (w_pad - w_hi.astype(F32)).astype(BF16)
    w_cat = jnp.concatenate([w_hi, w_lo], axis=1)
    kern = functools.partial(_router_kernel, experts=experts, slabs=slabs)
    return pl.pallas_call(
        kern,
        grid=(n // tm,),
        in_specs=[pl.BlockSpec((tm, d), lambda i: (i, 0)), pl.BlockSpec((1, d), lambda i: (0, 0)),
                  pl.BlockSpec((d, 2 * LANES), lambda i: (0, 0))],
        out_specs=[pl.BlockSpec((tm * slabs, LANES), lambda i: (i, 0)),
                   pl.BlockSpec((tm, LANES), lambda i: (i, 0)),
                   pl.BlockSpec((tm, LANES), lambda i: (i, 0)),
                   pl.BlockSpec((8, LANES), lambda i: (0, 0))],
        out_shape=[jax.ShapeDtypeStruct((n * slabs, LANES), jnp.uint32),
                   jax.ShapeDtypeStruct((n, LANES), jnp.int32),
                   jax.ShapeDtypeStruct((n, LANES), F32),
                   jax.ShapeDtypeStruct((8, LANES), jnp.int32)],
        scratch_shapes=[pltpu.VMEM((8, LANES), F32)],
        compiler_params=_params(("arbitrary",)),
        name="moe_router",
    )(x, g, w_cat)


def _pitch(slabs):
    groups = slabs // SUBLANES
    return slabs if groups % 2 == 1 else slabs + SUBLANES


def _row_copy(src_ref, dst_ref, sem, src_row, dst_row, slabs):
    pitch = _pitch(slabs)
    return pltpu.make_async_copy(
        src_ref.at[pl.ds(pl.multiple_of(src_row * slabs, slabs), slabs), :],
        dst_ref.at[pl.ds(pl.multiple_of(dst_row * pitch, SUBLANES), slabs), :],
        sem)


def _gather_rows(src_ref, dst_ref, sem, idx_ref, base, count, slabs, wait):
    def body(q, carry):
        for j in range(GATHER_UNROLL):
            r = q * GATHER_UNROLL + j
            cp = _row_copy(src_ref, dst_ref, sem, idx_ref[base + r], r, slabs)
            if wait:
                cp.wait()
            else:
                cp.start(priority=j % 2)
        return carry

    assert count % GATHER_UNROLL == 0
    lax.fori_loop(0, count // GATHER_UNROLL, body, 0)


def _dispatch_kernel(tok_ref, live_ref, h_ref, o_ref, gbuf_ref, sem, *, slabs):
    i = pl.program_id(0)
    last = pl.num_programs(0) - 1
    tr = o_ref.shape[0]
    slot = i % GATHER_DEPTH

    def fetch(tile, wait):
        s = tile % GATHER_DEPTH
        _gather_rows(h_ref, gbuf_ref.at[s], sem.at[s], tok_ref, tile * tr, tr, slabs, wait)

    def fetch_if_live(tile):
        @pl.when(jnp.logical_and(tile <= last, live_ref[jnp.minimum(tile, last)] == 1))
        def _():
            fetch(tile, False)

    @pl.when(i == 0)
    def _():
        for t in range(GATHER_DEPTH - 1):
            fetch_if_live(t)

    fetch_if_live(i + GATHER_DEPTH - 1)

    @pl.when(live_ref[i] == 1)
    def _():
        fetch(i, True)
        for f in range(slabs):
            lo, hi = _unpack_pairs(gbuf_ref[slot, pl.ds(f, tr, stride=_pitch(slabs)), :])
            o_ref[:, f * LANES:(f + 1) * LANES] = lo.astype(BF16)
            o_ref[:, (f + slabs) * LANES:(f + slabs + 1) * LANES] = hi.astype(BF16)

    @pl.when(live_ref[i] == 0)
    def _():
        o_ref[...] = jnp.zeros_like(o_ref)


def _dispatch(tok_of_slot, sub_live, h_slabs, d, tr):
    rows = tok_of_slot.shape[0]
    slabs = d // (2 * LANES)
    grid_spec = pltpu.PrefetchScalarGridSpec(
        num_scalar_prefetch=2,
        grid=(rows // tr,),
        in_specs=[pl.BlockSpec(memory_space=pl.ANY)],
        out_specs=pl.BlockSpec((tr, d), lambda i, tok, live: (i, 0)),
        scratch_shapes=[pltpu.VMEM((GATHER_DEPTH, tr * _pitch(slabs), LANES), jnp.uint32),
                        pltpu.SemaphoreType.DMA((GATHER_DEPTH,))],
    )
    return pl.pallas_call(
        functools.partial(_dispatch_kernel, slabs=slabs),
        grid_spec=grid_spec,
        out_shape=jax.ShapeDtypeStruct((rows, d), BF16),
        compiler_params=_params(("arbitrary",)),
        name="moe_dispatch",
    )(tok_of_slot, sub_live, h_slabs)


def _expert_ffn_kernel(pe_ref, ps_ref, pn_ref, pt_ref, used_ref, xs_ref, wg_ref, wu_ref, wd_ref, y_ref,
                       xv_ref, acc_ref, stage_ref, sem, *, tr, slabs):
    p = pl.program_id(0)
    c = pl.program_id(1)
    last = pl.num_programs(1) - 1
    nsub = pn_ref[p]
    sub0 = ps_ref[p]
    live = nsub > 0

    def rows_of(r, rows=tr):
        return pl.ds(pl.multiple_of(r * tr, tr), rows)

    def x_copy(r):
        return pltpu.make_async_copy(
            xs_ref.at[pl.ds(pl.multiple_of((sub0 + r) * tr, tr), tr), :],
            xv_ref.at[rows_of(r), :], sem.at[0])

    def x_start(r, carry):
        x_copy(r).start()
        return carry

    def x_wait(r, carry):
        x_copy(r).wait()
        return carry

    @pl.when(jnp.logical_and(live, c == 0))
    def _():
        lax.fori_loop(0, nsub, x_start, 0)
        lax.fori_loop(0, nsub, x_wait, 0)

    def hidden(r, rows):
        x = xv_ref[rows_of(r, rows), :]
        g = jnp.dot(x, wg_ref[...].astype(BF16), preferred_element_type=F32)
        u = jnp.dot(x, wu_ref[...].astype(BF16), preferred_element_type=F32)
        return (g * jax.nn.sigmoid(g) * u).astype(BF16)

    def y_copy(sub, slot):
        return pltpu.make_async_copy(
            stage_ref.at[slot],
            y_ref.at[pl.ds(pl.multiple_of(sub * tr * slabs, tr * slabs), tr * slabs), :],
            sem.at[1 + slot])

    def first_apply(r, slot, part):
        acc_ref[rows_of(r, part.shape[0]), :] = part

    def middle_apply(r, slot, part):
        acc_ref[rows_of(r, part.shape[0]), :] += part

    def final_apply(r, slot, part, reuse):
        rows = part.shape[0]
        res = acc_ref[rows_of(r, rows), :] + part
        if reuse:
            y_copy(sub0 + r - 2, slot).wait()
        for f in range(slabs):
            stage_ref[slot, pl.ds(f, rows, stride=slabs), :] = _pack_pairs(res, f)
        if rows < tr:
            stage_ref[slot, rows * slabs:, :] = jnp.zeros(((tr - rows) * slabs, LANES), jnp.uint32)
        y_copy(sub0 + r, slot).start()

    def for_each_subtile(apply, group, apply_first=None):
        def run(ap, first, count, rows=tr):
            hids = [hidden(first + t, rows) for t in range(count)]
            for t in range(count):
                part = jnp.dot(hids[t], wd_ref[...].astype(BF16), preferred_element_type=F32)
                ap(first + t, t % 2, part)

        def body(q, carry):
            run(apply, group * q, group)
            return carry

        peeled = 0 if apply_first is None else 1
        if peeled:
            @pl.when(nsub >= group)
            def _():
                run(apply_first, 0, group)

        lax.fori_loop(peeled, nsub // group, body, 0)
        rest = nsub % group
        size = group // 2
        while size > 1:
            @pl.when((rest // size) % 2 == 1)
            def _(size=size):
                run(apply, nsub - rest % (2 * size), size)
            size //= 2
        short = pt_ref[p] <= tr // 2
        lone = rest % 2 == 1
        tails = [(apply, lone)] if apply_first is None else [
            (apply, jnp.logical_and(lone, nsub > group)),
            (apply_first, jnp.logical_and(lone, nsub < group))]
        for ap, cond in tails:
            @pl.when(jnp.logical_and(cond, short))
            def _(ap=ap):
                run(ap, nsub - 1, 1, tr // 2)

            @pl.when(jnp.logical_and(cond, jnp.logical_not(short)))
            def _(ap=ap):
                run(ap, nsub - 1, 1)

    @pl.when(live)
    def _():
        @pl.when(c == 0)
        def _():
            for_each_subtile(first_apply, 2)

        @pl.when(jnp.logical_and(c > 0, c < last))
        def _():
            for_each_subtile(middle_apply, 4)

        @pl.when(c == last)
        def _():
            for_each_subtile(functools.partial(final_apply, reuse=True), 2,
                             functools.partial(final_apply, reuse=False))

            @pl.when(nsub >= 2)
            def _():
                y_copy(sub0 + nsub - 2, nsub % 2).wait()

            y_copy(sub0 + nsub - 1, (nsub - 1) % 2).wait()

    @pl.when(jnp.logical_and(p == pl.num_programs(0) - 1, c == last))
    def _():
        stage_ref[0] = jnp.zeros(stage_ref.shape[1:], jnp.uint32)

        def zero(s, carry):
            y_copy(s, 0).start()
            y_copy(s, 0).wait()
            return carry

        lax.fori_loop(used_ref[0], y_ref.shape[0] // (tr * slabs), zero, 0)


def _expert_ffn(xs, pass_table, used, w_gu, w_down, layer, tr, cap_sub, tc):
    rows, d = xs.shape
    fe = w_down.shape[2]
    nc = fe // tc
    assert nc >= 2
    slabs = d // (2 * LANES)
    last = nc - 1

    def chunk(c, pn, p):
        live = jnp.minimum(pn[p], 1)
        return c * live + last * (1 - live)

    grid_spec = pltpu.PrefetchScalarGridSpec(
        num_scalar_prefetch=5,
        grid=(pass_table[0].shape[0], nc),
        in_specs=[
            pl.BlockSpec(memory_space=pl.ANY),
            pl.BlockSpec((None, None, d, tc),
                         lambda p, c, pe, ps, pn, pt, us: (layer, pe[p], 0, chunk(c, pn, p))),
            pl.BlockSpec((None, None, d, tc),
                         lambda p, c, pe, ps, pn, pt, us: (layer, pe[p], 0, nc + chunk(c, pn, p))),
            pl.BlockSpec((None, None, tc, d),
                         lambda p, c, pe, ps, pn, pt, us: (layer, pe[p], chunk(c, pn, p), 0)),
        ],
        out_specs=pl.BlockSpec(memory_space=pl.ANY),
        scratch_shapes=[pltpu.VMEM((cap_sub * tr, d), BF16), pltpu.VMEM((cap_sub * tr, d), F32),
                        pltpu.VMEM((2, tr * slabs, LANES), jnp.uint32),
                        pltpu.SemaphoreType.DMA((3,))],
    )
    return pl.pallas_call(
        functools.partial(_expert_ffn_kernel, tr=tr, slabs=slabs),
        grid_spec=grid_spec,
        out_shape=jax.ShapeDtypeStruct((rows * slabs, LANES), jnp.uint32),
        compiler_params=_params(("arbitrary", "arbitrary")),
        name="moe_expert_swiglu",
    )(*pass_table, used, xs, w_gu, w_gu, w_down)


def _combine_kernel(s1_ref, s2_ref, y_ref, rw_ref, x_ref, g_ref, o_ref, b1_ref, b2_ref, sem,
                    *, slabs):
    i = pl.program_id(0)
    last = pl.num_programs(0) - 1
    tm = x_ref.shape[0]
    slot = i % GATHER_DEPTH

    def fetch(tile, wait):
        s = tile % GATHER_DEPTH
        _gather_rows(y_ref, b1_ref.at[s], sem.at[0, s], s1_ref, tile * tm, tm, slabs, wait)
        _gather_rows(y_ref, b2_ref.at[s], sem.at[1, s], s2_ref, tile * tm, tm, slabs, wait)

    @pl.when(i == 0)
    def _():
        for t in range(GATHER_DEPTH - 1):
            @pl.when(t <= last)
            def _(t=t):
                fetch(t, False)

    @pl.when(i + GATHER_DEPTH - 1 <= last)
    def _():
        fetch(i + GATHER_DEPTH - 1, False)

    fetch(i, True)

    rw = rw_ref[...]
    w1 = rw[:, 0:1]
    w2 = rw[:, 1:2]
    parts = [None] * (2 * slabs)
    ssq = jnp.zeros((tm, 1), F32)
    for f in range(slabs):
        lo1, hi1 = _unpack_pairs(b1_ref[slot, pl.ds(f, tm, stride=_pitch(slabs)), :])
        lo2, hi2 = _unpack_pairs(b2_ref[slot, pl.ds(f, tm, stride=_pitch(slabs)), :])
        for blk, a, b in ((f, lo1, lo2), (f + slabs, hi1, hi2)):
            m = w1 * a + w2 * b
            ssq = ssq + jnp.sum(m * m, axis=1, keepdims=True)
            parts[blk] = m
    inv = lax.rsqrt(ssq / (2 * slabs * LANES) + NORM_EPS)
    for blk in range(2 * slabs):
        sl = slice(blk * LANES, (blk + 1) * LANES)
        o_ref[:, sl] = x_ref[:, sl] + parts[blk] * inv * g_ref[:, sl]


def _combine(slot1, slot2, y_slabs, rw, x, g, tm):
    n, d = x.shape
    slabs = d // (2 * LANES)
    grid_spec = pltpu.PrefetchScalarGridSpec(
        num_scalar_prefetch=2,
        grid=(n // tm,),
        in_specs=[
            pl.BlockSpec(memory_space=pl.ANY),
            pl.BlockSpec((tm, LANES), lambda i, s1, s2: (i, 0)),
            pl.BlockSpec((tm, d), lambda i, s1, s2: (i, 0)),
            pl.BlockSpec((1, d), lambda i, s1, s2: (0, 0)),
        ],
        out_specs=pl.BlockSpec((tm, d), lambda i, s1, s2: (i, 0)),
        scratch_shapes=[pltpu.VMEM((GATHER_DEPTH, tm * _pitch(slabs), LANES), jnp.uint32),
                        pltpu.VMEM((GATHER_DEPTH, tm * _pitch(slabs), LANES), jnp.uint32),
                        pltpu.SemaphoreType.DMA((2, GATHER_DEPTH))],
    )
    return pl.pallas_call(
        functools.partial(_combine_kernel, slabs=slabs),
        grid_spec=grid_spec,
        out_shape=jax.ShapeDtypeStruct((n, d), F32),
        compiler_params=_params(("arbitrary",)),
        name="moe_combine",
    )(slot1, slot2, y_slabs, rw, x, g)


def _moe_ffn(x, g_in, g_out, w_router, w_gu, w_down, layer, tiles):
    n, d = x.shape
    experts = w_router.shape[1]
    assert d % (2 * LANES * SUBLANES) == 0
    tr = tiles["expert_rows"]
    cap_sub = tiles["expert_cap"] // tr
    h_slabs, ri, rw, cnt = _router(x, g_in, w_router, tiles["router_rows"])

    counts = cnt[0, :experts]
    nsub = (counts + tr - 1) // tr
    sub_end = jnp.cumsum(nsub)
    sub_start = sub_end - nsub
    slot1 = (sub_start[ri[:, 0]] * tr + ri[:, 2]).astype(jnp.int32)
    slot2 = (sub_start[ri[:, 1]] * tr + ri[:, 3]).astype(jnp.int32)
    rows = TOP_K * n + experts * tr
    n_sub = rows // tr
    token = jnp.arange(n, dtype=jnp.int32)
    tok_of_slot = jnp.zeros((rows,), jnp.int32).at[jnp.concatenate([slot1, slot2])].set(
        jnp.concatenate([token, token]), unique_indices=True)
    sub_live = (jnp.arange(n_sub, dtype=jnp.int32) < sub_end[-1]).astype(jnp.int32)

    n_pass = experts + n_sub // cap_sub
    passes = (nsub + cap_sub - 1) // cap_sub
    pass_end = jnp.cumsum(passes)
    pidx = jnp.arange(n_pass, dtype=jnp.int32)
    pe = jnp.minimum(jnp.searchsorted(pass_end, pidx, side="right"), experts - 1)
    local = pidx - (pass_end[pe] - passes[pe])
    p_live = pidx < pass_end[-1]
    pass_nsub = jnp.where(p_live, jnp.minimum(cap_sub, nsub[pe] - local * cap_sub), 0)
    pass_sub0 = jnp.where(p_live, sub_start[pe] + local * cap_sub, 0)
    last_pe = jnp.max(jnp.where(passes > 0, jnp.arange(experts), 0))
    pass_expert = jnp.where(p_live, pe, last_pe)
    pass_tokens = jnp.minimum(cap_sub * tr, counts[pe] - local * cap_sub * tr)
    pass_tail = jnp.where(p_live, pass_tokens - (pass_nsub - 1) * tr, 0)
    pass_table = tuple(a.astype(jnp.int32) for a in (pass_expert, pass_sub0, pass_nsub, pass_tail))

    xs = _dispatch(tok_of_slot, sub_live, h_slabs, d, tr)
    y_slabs = _expert_ffn(xs, pass_table, sub_end[-1:].astype(jnp.int32), w_gu, w_down, layer, tr,
                          cap_sub, tiles["expert_chunk"])
    return _combine(slot1, slot2, y_slabs, rw, x, g_out, tiles["combine_rows"])


def _tiles(n, seq, d, ffn, expert_ffn):
    pick = lambda pref, dim: min(pref, dim)
    return {
        "conv_rows": pick(256, seq), "conv_cols": pick(512, d),
        "ln_rows": pick(256, n),
        "ffn_rows": pick(512, n), "ffn_chunk": pick(512, ffn),
        "proj_rows": pick(256, seq),
        "attn_q": pick(512, seq), "attn_k": pick(512, seq),
        "out_rows": pick(256, n),
        "router_rows": pick(512, n),
        "expert_rows": pick(256, n), "expert_cap": pick(2304, TOP_K * n),
        "expert_chunk": pick(256, expert_ffn),
        "combine_rows": pick(256, n),
    }


def _rope_tables(seq):
    pos = jnp.arange(seq, dtype=F32)
    inv_freq = ROPE_THETA ** (-jnp.arange(0, QK_ROPE_DIM, 2, dtype=F32) / QK_ROPE_DIM)
    ang = pos[:, None] * inv_freq[None, :]
    cos, sin = jnp.cos(ang), jnp.sin(ang)
    zeros = jnp.zeros((seq, LANES - QK_ROPE_DIM), F32)
    return (jnp.concatenate([cos, cos, zeros], axis=1),
            jnp.concatenate([-sin, sin, zeros], axis=1))


def _swap_halves(w):
    half = w.shape[-1] // 2
    return jnp.concatenate([w[..., half:], w[..., :half]], axis=-1)


def kernel(x, norm_g, conv_w_pw1, conv_b_pw1, conv_w_dw, conv_b_dw, conv_ln_g, conv_ln_b,
           conv_w_pw2, ffn_w_gu, ffn_w_down, moe_w_router, moe_w_gu, moe_w_down,
           kv_in_g, kv_w_a, kv_latent_g, kv_w_b, attn_w_q_a, attn_q_latent_g,
           attn_w_q_b, attn_w_o):
    batch, seq, d = x.shape
    n = batch * seq
    depth = norm_g.shape[0]
    n_conv = depth // 2
    heads = d // 128
    kv_rank = kv_latent_g.shape[0]
    tiles = _tiles(n, seq, d, ffn_w_down.shape[1], moe_w_down.shape[2])
    cos2, sin2 = _rope_tables(seq)
    vec = lambda v: v.reshape(1, -1)

    xs = x.reshape(n, d)
    k = v = None
    for i in range(depth):
        if i < n_conv:
            c = _pw1_dwconv(xs, vec(norm_g[i, 0]), conv_w_pw1[i].astype(BF16), vec(conv_b_pw1[i]),
                            conv_w_dw[i], vec(conv_b_dw[i]), seq, tiles["conv_rows"],
                            tiles["conv_cols"])
            xs = _ln_pw2(c, xs, vec(conv_ln_g[i]), vec(conv_ln_b[i]), conv_w_pw2[i].astype(BF16),
                         vec(norm_g[i, 1]), tiles["ln_rows"])
        else:
            j = i - n_conv
            q_rank = attn_w_q_a.shape[2]
            wqb = attn_w_q_b[j].reshape(q_rank, heads, QK_NOPE_DIM + QK_ROPE_DIM)
            rope_w = wqb[:, :, QK_NOPE_DIM:]
            wqb_ext = jnp.concatenate([wqb, _swap_halves(rope_w)], axis=2)
            wqb_ext = wqb_ext.reshape(q_rank, heads * 256).astype(BF16)
            q_args = (attn_w_q_a[j].astype(BF16), vec(attn_q_latent_g[j]), wqb_ext, cos2, sin2,
                      seq, tiles["proj_rows"])
            if j == 0:
                rope_w = kv_w_a[:, kv_rank:]
                wa_ext = jnp.concatenate([kv_w_a, _swap_halves(rope_w)], axis=1).astype(BF16)
                wb = kv_w_b.reshape(kv_rank, heads, QK_NOPE_DIM + V_DIM)
                wb_split = jnp.concatenate(
                    [wb[:, :, :QK_NOPE_DIM].reshape(kv_rank, heads * QK_NOPE_DIM),
                     wb[:, :, QK_NOPE_DIM:].reshape(kv_rank, heads * V_DIM)], axis=1).astype(BF16)
                k, v, q = _qkv_proj(xs, vec(kv_in_g), vec(norm_g[i, 0]), wa_ext, vec(kv_latent_g),
                                    wb_split, *q_args)
            else:
                q = _q_proj(xs, vec(norm_g[i, 0]), *q_args)
            o = _attention(q, k, v, batch, seq, tiles["attn_q"], tiles["attn_k"])
            xs = _attn_out(o, xs, attn_w_o[j].astype(BF16), vec(norm_g[i, 1]), tiles["out_rows"])
        if i % 2 == 0:
            xs = _dense_ffn(xs, vec(norm_g[i, 2]), vec(norm_g[i, 3]), ffn_w_gu[i // 2].astype(BF16),
                            ffn_w_down[i // 2].astype(BF16), tiles["ffn_rows"], tiles["ffn_chunk"])
        else:
            xs = _moe_ffn(xs, vec(norm_g[i, 2]), vec(norm_g[i, 3]), moe_w_router[i // 2],
                          moe_w_gu, moe_w_down, i // 2, tiles)
    return xs.reshape(batch, seq, d)
```

```python
import functools

import jax
import jax.numpy as jnp
from jax import lax
from jax.experimental import pallas as pl
from jax.experimental.pallas import tpu as pltpu

NORM_EPS = 1e-6
ROPE_THETA = 10000.0
QK_NOPE_DIM = 128
QK_ROPE_DIM = 64
V_DIM = 128
TOP_K = 2
_Q_SCALE = float(QK_NOPE_DIM + QK_ROPE_DIM) ** -0.5 * 1.4426950408889634

LANES = 128
SUBLANES = 8
CONV_HALO = 32
GATHER_UNROLL = 8
GATHER_DEPTH = 3
VMEM_LIMIT = 56 * 1024 * 1024

F32 = jnp.float32
BF16 = jnp.bfloat16


def _params(semantics):
    return pltpu.CompilerParams(dimension_semantics=semantics, vmem_limit_bytes=VMEM_LIMIT)


def _rms(x):
    return x * lax.rsqrt(jnp.mean(x * x, axis=-1, keepdims=True) + NORM_EPS)


_HIGH_HALF = 0xFFFF0000


def _pack_pairs(x, f):
    half = x.shape[1] // (2 * LANES)
    lo = x[:, f * LANES:(f + 1) * LANES].astype(BF16).astype(F32)
    hi = x[:, (f + half) * LANES:(f + half + 1) * LANES].astype(BF16).astype(F32)
    lo_bits = lax.shift_right_logical(lax.bitcast_convert_type(lo, jnp.uint32), jnp.uint32(16))
    hi_bits = lax.bitcast_convert_type(hi, jnp.uint32) & jnp.uint32(_HIGH_HALF)
    return hi_bits | lo_bits


def _unpack_pairs(w):
    lo = lax.bitcast_convert_type(lax.shift_left(w, jnp.uint32(16)), F32)
    hi = lax.bitcast_convert_type(w & jnp.uint32(_HIGH_HALF), F32)
    return lo, hi


def _pw1_dwconv_kernel(x_ref, g_ref, w_ref, b_ref, wdw_ref, bdw_ref, o_ref,
                       xn_ref, carry_ref, sh_ref, *, tiles_per_seq, tn, rb, cb):
    tm, d = x_ref.shape
    width = wdw_ref.shape[0]
    ext = tm + CONV_HALO
    base = CONV_HALO - (width - 1)
    first = (pl.program_id(0) % tiles_per_seq) == 0
    xn_ref[...] = (_rms(x_ref[...]) * g_ref[...]).astype(BF16)

    @pl.when(pl.program_id(0) == 0)
    def _():
        carry_ref[...] = jnp.zeros_like(carry_ref)

    for j in range(d // tn):
        cols = slice(j * tn, (j + 1) * tn)
        gate_cols = slice(d + j * tn, d + (j + 1) * tn)
        a = jnp.dot(xn_ref[...], w_ref[:, cols], preferred_element_type=F32) + b_ref[:, cols]
        g = jnp.dot(xn_ref[...], w_ref[:, gate_cols], preferred_element_type=F32) + b_ref[:, gate_cols]
        u = a * jax.nn.sigmoid(g)
        sh = sh_ref.at[j % 2]
        halo = carry_ref[:, cols]
        sh[0, 0:CONV_HALO, :] = jnp.where(first, jnp.zeros_like(halo), halo)
        sh[0, CONV_HALO:ext, :] = u
        carry_ref[:, cols] = u[tm - CONV_HALO:, :]
        for s in range(1, SUBLANES):
            sh[s, 0:ext - SUBLANES, :] = sh[0, s:ext - SUBLANES + s, :]
        for r0 in range(0, tm, rb):
            for c0 in range(0, tn, cb):
                acc = None
                for k in range(width):
                    s = (base + k) % SUBLANES
                    off = (base + k) - s
                    term = (wdw_ref[k:k + 1, j * tn + c0:j * tn + c0 + cb]
                            * sh[s, r0 + off:r0 + off + rb, c0:c0 + cb])
                    acc = term if acc is None else acc + term
                o_ref[r0:r0 + rb, j * tn + c0:j * tn + c0 + cb] = (
                    acc + bdw_ref[:, j * tn + c0:j * tn + c0 + cb]).astype(o_ref.dtype)


def _pw1_dwconv(x, g, w, b, w_dw, b_dw, seq, tm, tn):
    n, d = x.shape
    width = w_dw.shape[0]
    assert width - 1 <= CONV_HALO <= tm
    kern = functools.partial(_pw1_dwconv_kernel, tiles_per_seq=seq // tm, tn=tn,
                             rb=min(64, tm), cb=min(256, tn))
    const = lambda shape: pl.BlockSpec(shape, lambda i: (0, 0), pipeline_mode=pl.Buffered(1))
    return pl.pallas_call(
        kern,
        grid=(n // tm,),
        in_specs=[pl.BlockSpec((tm, d), lambda i: (i, 0)), const((1, d)), const(w.shape),
                  const(b.shape), const((width, d)), const((1, d))],
        out_specs=pl.BlockSpec((tm, d), lambda i: (i, 0)),
        out_shape=jax.ShapeDtypeStruct((n, d), BF16),
        scratch_shapes=[pltpu.VMEM((tm, d), BF16), pltpu.VMEM((CONV_HALO, d), F32),
                        pltpu.VMEM((2, SUBLANES, tm + CONV_HALO, tn), F32)],
        compiler_params=_params(("arbitrary",)),
        name="conv_pw1_dwconv",
    )(x, g, w, b, w_dw, b_dw)


def _ln_pw2_kernel(c_ref, x_ref, lg_ref, lb_ref, w_ref, g_ref, o_ref):
    c = c_ref[...].astype(F32)
    mu = jnp.mean(c, axis=-1, keepdims=True)
    cc = c - mu
    y = cc * lax.rsqrt(jnp.mean(cc * cc, axis=-1, keepdims=True) + NORM_EPS)
    y = y * lg_ref[...] + lb_ref[...]
    y = (y * jax.nn.sigmoid(y)).astype(BF16)
    m = jnp.dot(y, w_ref[...], preferred_element_type=F32)
    o_ref[...] = x_ref[...] + _rms(m) * g_ref[...]


def _ln_pw2(c, x, ln_g, ln_b, w, g, tm):
    n, d = x.shape
    row = pl.BlockSpec((tm, d), lambda i: (i, 0))
    vec = pl.BlockSpec((1, d), lambda i: (0, 0))
    return pl.pallas_call(
        _ln_pw2_kernel,
        grid=(n // tm,),
        in_specs=[row, row, vec, vec, pl.BlockSpec((d, d), lambda i: (0, 0)), vec],
        out_specs=row,
        out_shape=jax.ShapeDtypeStruct((n, d), F32),
        compiler_params=_params(("parallel",)),
        name="conv_ln_pw2",
    )(c, x, ln_g, ln_b, w, g)


def _swiglu_chunk(xn_ref, wg_ref, wu_ref, wd_ref, acc_ref, groups):
    rows = xn_ref.shape[0] // groups
    hids = []
    for i in range(groups):
        xn = xn_ref[i * rows:(i + 1) * rows, :]
        g = jnp.dot(xn, wg_ref[...], preferred_element_type=F32)
        u = jnp.dot(xn, wu_ref[...], preferred_element_type=F32)
        hids.append((g * jax.nn.sigmoid(g) * u).astype(BF16))
    for i in range(groups):
        acc_ref[i * rows:(i + 1) * rows, :] += jnp.dot(hids[i], wd_ref[...],
                                                       preferred_element_type=F32)


def _dense_ffn_kernel(x_ref, gi_ref, go_ref, wg_ref, wu_ref, wd_ref, o_ref, xn_ref, acc_ref):
    c = pl.program_id(1)

    @pl.when(c == 0)
    def _():
        xn_ref[...] = (_rms(x_ref[...]) * gi_ref[...]).astype(BF16)
        acc_ref[...] = jnp.zeros_like(acc_ref)

    _swiglu_chunk(xn_ref, wg_ref, wu_ref, wd_ref, acc_ref, groups=2)

    @pl.when(c == pl.num_programs(1) - 1)
    def _():
        o_ref[...] = x_ref[...] + _rms(acc_ref[...]) * go_ref[...]


def _dense_ffn(x, g_in, g_out, w_gu, w_down, tm, tc):
    n, d = x.shape
    f = w_down.shape[0]
    nc = f // tc
    row = pl.BlockSpec((tm, d), lambda i, c: (i, 0))
    vec = pl.BlockSpec((1, d), lambda i, c: (0, 0))
    return pl.pallas_call(
        _dense_ffn_kernel,
        grid=(n // tm, nc),
        in_specs=[
            row, vec, vec,
            pl.BlockSpec((d, tc), lambda i, c: (0, c)),
            pl.BlockSpec((d, tc), lambda i, c: (0, nc + c)),
            pl.BlockSpec((tc, d), lambda i, c: (c, 0)),
        ],
        out_specs=row,
        out_shape=jax.ShapeDtypeStruct((n, d), F32),
        scratch_shapes=[pltpu.VMEM((tm, d), BF16), pltpu.VMEM((tm, d), F32)],
        compiler_params=_params(("parallel", "arbitrary")),
        name="dense_swiglu",
    )(x, g_in, g_out, w_gu, w_gu, w_down)


def _rope128(t, cos2, sin2):
    return t * cos2 + pltpu.roll(t, 64, axis=1) * sin2


def _q_proj_kernel(x_ref, g_ref, wa_ref, lg_ref, wb_ref, cos_ref, sin_ref, q_ref, *, heads, scale):
    h = (_rms(x_ref[...]) * g_ref[...]).astype(BF16)
    lat = jnp.dot(h, wa_ref[...], preferred_element_type=F32)
    lat = (_rms(lat) * lg_ref[...]).astype(BF16)
    q = jnp.dot(lat, wb_ref[...], preferred_element_type=F32)
    cos2 = cos_ref[...]
    sin2 = sin_ref[...]
    for hh in range(heads):
        q_ref[:, hh * 256:hh * 256 + 128] = (q[:, hh * 256:hh * 256 + 128] * scale).astype(BF16)
        r = _rope128(q[:, hh * 256 + 128:(hh + 1) * 256], cos2, sin2)
        q_ref[:, hh * 256 + 128:(hh + 1) * 256] = (r * scale).astype(BF16)


def _q_proj(x, g, wa, lat_g, wb_ext, cos2, sin2, seq, tm):
    n, d = x.shape
    heads = d // 128
    spt = seq // tm
    kern = functools.partial(_q_proj_kernel, heads=heads, scale=_Q_SCALE)
    full = lambda a: pl.BlockSpec(a.shape, lambda i: (0, 0))
    return pl.pallas_call(
        kern,
        grid=(n // tm,),
        in_specs=[
            pl.BlockSpec((tm, d), lambda i: (i, 0)), full(g), full(wa), full(lat_g), full(wb_ext),
            pl.BlockSpec((tm, LANES), lambda i: (i % spt, 0)),
            pl.BlockSpec((tm, LANES), lambda i: (i % spt, 0)),
        ],
        out_specs=pl.BlockSpec((tm, heads * 256), lambda i: (i, 0)),
        out_shape=jax.ShapeDtypeStruct((n, heads * 256), BF16),
        compiler_params=_params(("parallel",)),
        name="mla_q_proj",
    )(x, g, wa, lat_g, wb_ext, cos2, sin2)


def _qkv_proj_kernel(x_ref, gkv_ref, gq_ref, wkva_ref, lkv_ref, wkvb_ref, wqa_ref, lq_ref, wqb_ref,
                     cos_ref, sin_ref, k_ref, v_ref, q_ref, *, heads, rank, scale):
    xhat = _rms(x_ref[...])
    h_kv = (xhat * gkv_ref[...]).astype(BF16)
    h_q = (xhat * gq_ref[...]).astype(BF16)
    a = jnp.dot(h_kv, wkva_ref[...], preferred_element_type=F32)
    lat = jnp.dot(h_q, wqa_ref[...], preferred_element_type=F32)
    c_kv = (_rms(a[:, :rank]) * lkv_ref[...]).astype(BF16)
    lat = (_rms(lat) * lq_ref[...]).astype(BF16)
    cos2 = cos_ref[...]
    sin2 = sin_ref[...]
    k_rope = _rope128(a[:, rank:], cos2, sin2).astype(BF16)
    kv = jnp.dot(c_kv, wkvb_ref[...], preferred_element_type=F32)
    q = jnp.dot(lat, wqb_ref[...], preferred_element_type=F32)
    hd = heads * QK_NOPE_DIM
    for hh in range(heads):
        k_ref[:, hh * 256:hh * 256 + 128] = kv[:, hh * 128:(hh + 1) * 128].astype(BF16)
        k_ref[:, hh * 256 + 128:(hh + 1) * 256] = k_rope
    v_ref[...] = kv[:, hd:].astype(BF16)
    for hh in range(heads):
        q_ref[:, hh * 256:hh * 256 + 128] = (q[:, hh * 256:hh * 256 + 128] * scale).astype(BF16)
        r = _rope128(q[:, hh * 256 + 128:(hh + 1) * 256], cos2, sin2)
        q_ref[:, hh * 256 + 128:(hh + 1) * 256] = (r * scale).astype(BF16)


def _qkv_proj(x, g_kv, g_q, wkva_ext, lat_kv, wkvb_split, wqa, lat_q, wqb_ext, cos2, sin2, seq, tm):
    n, d = x.shape
    heads = d // 128
    spt = seq // tm
    kern = functools.partial(_qkv_proj_kernel, heads=heads, rank=lat_kv.shape[1], scale=_Q_SCALE)
    full = lambda a: pl.BlockSpec(a.shape, lambda i: (0, 0))
    rope = pl.BlockSpec((tm, LANES), lambda i: (i % spt, 0))
    wide = pl.BlockSpec((tm, heads * 256), lambda i: (i, 0))
    return pl.pallas_call(
        kern,
        grid=(n // tm,),
        in_specs=[pl.BlockSpec((tm, d), lambda i: (i, 0)), full(g_kv), full(g_q), full(wkva_ext),
                  full(lat_kv), full(wkvb_split), full(wqa), full(lat_q), full(wqb_ext), rope, rope],
        out_specs=[wide, pl.BlockSpec((tm, heads * 128), lambda i: (i, 0)), wide],
        out_shape=[jax.ShapeDtypeStruct((n, heads * 256), BF16),
                   jax.ShapeDtypeStruct((n, heads * 128), BF16),
                   jax.ShapeDtypeStruct((n, heads * 256), BF16)],
        compiler_params=_params(("parallel",)),
        name="mla_qkv_proj",
    )(x, g_kv, g_q, wkva_ext, lat_kv, wkvb_split, wqa, lat_q, wqb_ext, cos2, sin2)


def _attn_kernel(q_ref, k_ref, v_ref, o_ref, *, tq, tk):
    seq = q_ref.shape[0]
    row = lax.broadcasted_iota(jnp.int32, (tq, tk), 0)
    col = lax.broadcasted_iota(jnp.int32, (tq, tk), 1)
    for qi in range(seq // tq):
        q = q_ref[qi * tq:(qi + 1) * tq, :]
        m = jnp.full((tq, 1), -jnp.inf, F32)
        l = jnp.zeros((tq, 1), F32)
        acc = jnp.zeros((tq, V_DIM), F32)
        for j in range((qi + 1) * tq // tk):
            k = k_ref[j * tk:(j + 1) * tk, :]
            s = lax.dot_general(q, k, (((1,), (1,)), ((), ())), preferred_element_type=F32)
            if (j + 1) * tk > qi * tq + 1:
                s = jnp.where(row + qi * tq >= col + j * tk, s, -jnp.inf)
            m_new = jnp.maximum(m, jnp.max(s, axis=1, keepdims=True))
            p = jnp.exp2(s - m_new)
            alpha = jnp.exp2(m - m_new)
            l = alpha * l + jnp.sum(p, axis=1, keepdims=True)
            pv = jnp.dot(p.astype(BF16), v_ref[j * tk:(j + 1) * tk, :],
                         preferred_element_type=F32)
            acc = alpha * acc + pv
            m = m_new
        o_ref[qi * tq:(qi + 1) * tq, :] = (acc / l).astype(o_ref.dtype)


def _attention(q, k, v, batch, seq, tq, tk):
    n = q.shape[0]
    heads = v.shape[1] // V_DIM
    kern = functools.partial(_attn_kernel, tq=tq, tk=tk)
    return pl.pallas_call(
        kern,
        grid=(batch, heads),
        in_specs=[
            pl.BlockSpec((seq, 256), lambda b, h: (b, h)),
            pl.BlockSpec((seq, 256), lambda b, h: (b, h)),
            pl.BlockSpec((seq, V_DIM), lambda b, h: (b, h)),
        ],
        out_specs=pl.BlockSpec((seq, V_DIM), lambda b, h: (b, h)),
        out_shape=jax.ShapeDtypeStruct((n, heads * V_DIM), BF16),
        compiler_params=_params(("parallel", "parallel")),
        name="mla_attention",
    )(q, k, v)


def _attn_out_kernel(o_ref, x_ref, w_ref, g_ref, xo_ref):
    m = jnp.dot(o_ref[...], w_ref[...], preferred_element_type=F32)
    xo_ref[...] = x_ref[...] + _rms(m) * g_ref[...]


def _attn_out(o, x, w, g, tm):
    n, d = x.shape
    row = pl.BlockSpec((tm, d), lambda i: (i, 0))
    return pl.pallas_call(
        _attn_out_kernel,
        grid=(n // tm,),
        in_specs=[pl.BlockSpec((tm, o.shape[1]), lambda i: (i, 0)), row,
                  pl.BlockSpec(w.shape, lambda i: (0, 0)), pl.BlockSpec((1, d), lambda i: (0, 0))],
        out_specs=row,
        out_shape=jax.ShapeDtypeStruct((n, d), F32),
        compiler_params=_params(("parallel",)),
        name="mla_out_proj",
    )(o, x, w, g)


def _router_kernel(x_ref, g_ref, wcat_ref, h_ref, ri_ref, rw_ref, cnt_ref, base_ref,
                   *, experts, slabs):
    tm = x_ref.shape[0]

    @pl.when(pl.program_id(0) == 0)
    def _():
        base_ref[...] = jnp.zeros_like(base_ref)

    h = _rms(x_ref[...]) * g_ref[...]
    for f in range(slabs):
        h_ref[pl.ds(f, tm, stride=slabs), :] = _pack_pairs(h, f)

    h_hi = h.astype(BF16)
    h_lo = (h - h_hi.astype(F32)).astype(BF16)
    hi = jnp.dot(h_hi, wcat_ref[...], preferred_element_type=F32)
    logits = (hi[:, :LANES] + hi[:, LANES:]
              + jnp.dot(h_lo, wcat_ref[:, :LANES], preferred_element_type=F32))

    lane = lax.broadcasted_iota(jnp.int32, (tm, LANES), 1)
    lg = jnp.where(lane < experts, logits, -jnp.inf)
    m1 = jnp.max(lg, axis=1, keepdims=True)
    i1 = jnp.min(jnp.where(lg == m1, lane, LANES), axis=1, keepdims=True)
    oh1 = lane == i1
    lg2 = jnp.where(oh1, -jnp.inf, lg)
    m2 = jnp.max(lg2, axis=1, keepdims=True)
    i2 = jnp.min(jnp.where(lg2 == m2, lane, LANES), axis=1, keepdims=True)
    oh2 = lane == i2
    e2 = jnp.exp(m2 - m1)
    w1 = 1.0 / (1.0 + e2)
    w2 = e2 / (1.0 + e2)

    cnt = oh1.astype(F32) + oh2.astype(F32)
    r_i = lax.broadcasted_iota(jnp.int32, (tm, tm), 0)
    c_i = lax.broadcasted_iota(jnp.int32, (tm, tm), 1)
    tri = (c_i < r_i).astype(BF16)
    before = jnp.dot(tri, cnt.astype(BF16), preferred_element_type=F32) + base_ref[0:1, :]
    rank1 = jnp.sum(jnp.where(oh1, before, 0.0), axis=1, keepdims=True).astype(jnp.int32)
    rank2 = jnp.sum(jnp.where(oh2, before, 0.0), axis=1, keepdims=True).astype(jnp.int32)
    total = base_ref[0:1, :] + jnp.sum(cnt, axis=0, keepdims=True)
    base_ref[...] = jnp.broadcast_to(total, base_ref.shape)
    cnt_ref[...] = jnp.broadcast_to(total, cnt_ref.shape).astype(jnp.int32)

    ri_ref[...] = jnp.where(lane == 0, i1, jnp.where(lane == 1, i2,
                            jnp.where(lane == 2, rank1, jnp.where(lane == 3, rank2, 0))))
    rw_ref[...] = jnp.where(lane == 0, w1, jnp.where(lane == 1, w2, 0.0))


def _router(x, g, w_router, tm):
    n, d = x.shape
    experts = w_router.shape[1]
    slabs = d // (2 * LANES)
    w_pad = jnp.zeros((d, LANES), F32).at[:, :experts].set(w_router)
    w_hi = w_pad.astype(BF16)
    w_lo = (w_pad - w_hi.astype(F32)).astype(BF16)
    w_cat = jnp.concatenate([w_hi, w_lo], axis=1)
    kern = functools.partial(_router_kernel, experts=experts, slabs=slabs)
    return pl.pallas_call(
        kern,
        grid=(n // tm,),
        in_specs=[pl.BlockSpec((tm, d), lambda i: (i, 0)), pl.BlockSpec((1, d), lambda i: (0, 0)),
                  pl.BlockSpec((d, 2 * LANES), lambda i: (0, 0))],
        out_specs=[pl.BlockSpec((tm * slabs, LANES), lambda i: (i, 0)),
                   pl.BlockSpec((tm, LANES), lambda i: (i, 0)),
                   pl.BlockSpec((tm, LANES), lambda i: (i, 0)),
                   pl.BlockSpec((8, LANES), lambda i: (0, 0))],
        out_shape=[jax.ShapeDtypeStruct((n * slabs, LANES), jnp.uint32),
                   jax.ShapeDtypeStruct((n, LANES), jnp.int32),
                   jax.ShapeDtypeStruct((n, LANES), F32),
                   jax.ShapeDtypeStruct((8, LANES), jnp.int32)],
        scratch_shapes=[pltpu.VMEM((8, LANES), F32)],
        compiler_params=_params(("arbitrary",)),
        name="moe_router",
    )(x, g, w_cat)


def _pitch(slabs):
    groups = slabs // SUBLANES
    return slabs if groups % 2 == 1 else slabs + SUBLANES


def _row_copy(src_ref, dst_ref, sem, src_row, dst_row, slabs):
    pitch = _pitch(slabs)
    return pltpu.make_async_copy(
        src_ref.at[pl.ds(pl.multiple_of(src_row * slabs, slabs), slabs), :],
        dst_ref.at[pl.ds(pl.multiple_of(dst_row * pitch, SUBLANES), slabs), :],
        sem)


def _gather_rows(src_ref, dst_ref, sem, idx_ref, base, count, slabs, wait):
    def body(q, carry):
        for j in range(GATHER_UNROLL):
            r = q * GATHER_UNROLL + j
            cp = _row_copy(src_ref, dst_ref, sem, idx_ref[base + r], r, slabs)
            if wait:
                cp.wait()
            else:
                cp.start(priority=j % 2)
        return carry

    assert count % GATHER_UNROLL == 0
    lax.fori_loop(0, count // GATHER_UNROLL, body, 0)


def _dispatch_kernel(tok_ref, live_ref, h_ref, o_ref, gbuf_ref, sem, *, slabs):
    i = pl.program_id(0)
    last = pl.num_programs(0) - 1
    tr = o_ref.shape[0]
    slot = i % GATHER_DEPTH

    def fetch(tile, wait):
        s = tile % GATHER_DEPTH
        _gather_rows(h_ref, gbuf_ref.at[s], sem.at[s], tok_ref, tile * tr, tr, slabs, wait)

    def fetch_if_live(tile):
        @pl.when(jnp.logical_and(tile <= last, live_ref[jnp.minimum(tile, last)] == 1))
        def _():
            fetch(tile, False)

    @pl.when(i == 0)
    def _():
        for t in range(GATHER_DEPTH - 1):
            fetch_if_live(t)

    fetch_if_live(i + GATHER_DEPTH - 1)

    @pl.when(live_ref[i] == 1)
    def _():
        fetch(i, True)
        for f in range(slabs):
            lo, hi = _unpack_pairs(gbuf_ref[slot, pl.ds(f, tr, stride=_pitch(slabs)), :])
            o_ref[:, f * LANES:(f + 1) * LANES] = lo.astype(BF16)
            o_ref[:, (f + slabs) * LANES:(f + slabs + 1) * LANES] = hi.astype(BF16)

    @pl.when(live_ref[i] == 0)
    def _():
        o_ref[...] = jnp.zeros_like(o_ref)


def _dispatch(tok_of_slot, sub_live, h_slabs, d, tr):
    rows = tok_of_slot.shape[0]
    slabs = d // (2 * LANES)
    grid_spec = pltpu.PrefetchScalarGridSpec(
        num_scalar_prefetch=2,
        grid=(rows // tr,),
        in_specs=[pl.BlockSpec(memory_space=pl.ANY)],
        out_specs=pl.BlockSpec((tr, d), lambda i, tok, live: (i, 0)),
        scratch_shapes=[pltpu.VMEM((GATHER_DEPTH, tr * _pitch(slabs), LANES), jnp.uint32),
                        pltpu.SemaphoreType.DMA((GATHER_DEPTH,))],
    )
    return pl.pallas_call(
        functools.partial(_dispatch_kernel, slabs=slabs),
        grid_spec=grid_spec,
        out_shape=jax.ShapeDtypeStruct((rows, d), BF16),
        compiler_params=_params(("arbitrary",)),
        name="moe_dispatch",
    )(tok_of_slot, sub_live, h_slabs)


def _expert_ffn_kernel(pe_ref, ps_ref, pn_ref, pt_ref, used_ref, xs_ref, wg_ref, wu_ref, wd_ref, y_ref,
                       xv_ref, acc_ref, stage_ref, sem, *, tr, slabs):
    p = pl.program_id(0)
    c = pl.program_id(1)
    last = pl.num_programs(1) - 1
    nsub = pn_ref[p]
    sub0 = ps_ref[p]
    live = nsub > 0

    def rows_of(r, rows=tr):
        return pl.ds(pl.multiple_of(r * tr, tr), rows)

    def x_copy(r):
        return pltpu.make_async_copy(
            xs_ref.at[pl.ds(pl.multiple_of((sub0 + r) * tr, tr), tr), :],
            xv_ref.at[rows_of(r), :], sem.at[0])

    def x_start(r, carry):
        x_copy(r).start()
        return carry

    def x_wait(r, carry):
        x_copy(r).wait()
        return carry

    @pl.when(jnp.logical_and(live, c == 0))
    def _():
        lax.fori_loop(0, nsub, x_start, 0)
        lax.fori_loop(0, nsub, x_wait, 0)

    def hidden(r, rows):
        x = xv_ref[rows_of(r, rows), :]
        g = jnp.dot(x, wg_ref[...].astype(BF16), preferred_element_type=F32)
        u = jnp.dot(x, wu_ref[...].astype(BF16), preferred_element_type=F32)
        return (g * jax.nn.sigmoid(g) * u).astype(BF16)

    def y_copy(sub, slot):
        return pltpu.make_async_copy(
            stage_ref.at[slot],
            y_ref.at[pl.ds(pl.multiple_of(sub * tr * slabs, tr * slabs), tr * slabs), :],
            sem.at[1 + slot])

    def first_apply(r, slot, part):
        acc_ref[rows_of(r, part.shape[0]), :] = part

    def middle_apply(r, slot, part):
        acc_ref[rows_of(r, part.shape[0]), :] += part

    def final_apply(r, slot, part, reuse):
        rows = part.shape[0]
        res = acc_ref[rows_of(r, rows), :] + part
        if reuse:
            y_copy(sub0 + r - 2, slot).wait()
        for f in range(slabs):
            stage_ref[slot, pl.ds(f, rows, stride=slabs), :] = _pack_pairs(res, f)
        if rows < tr:
            stage_ref[slot, rows * slabs:, :] = jnp.zeros(((tr - rows) * slabs, LANES), jnp.uint32)
        y_copy(sub0 + r, slot).start()

    def for_each_subtile(apply, group, apply_first=None):
        def run(ap, first, count, rows=tr):
            hids = [hidden(first + t, rows) for t in range(count)]
            for t in range(count):
                part = jnp.dot(hids[t], wd_ref[...].astype(BF16), preferred_element_type=F32)
                ap(first + t, t % 2, part)

        def body(q, carry):
            run(apply, group * q, group)
            return carry

        peeled = 0 if apply_first is None else 1
        if peeled:
            @pl.when(nsub >= group)
            def _():
                run(apply_first, 0, group)

        lax.fori_loop(peeled, nsub // group, body, 0)
        rest = nsub % group
        size = group // 2
        while size > 1:
            @pl.when((rest // size) % 2 == 1)
            def _(size=size):
                run(apply, nsub - rest % (2 * size), size)
            size //= 2
        short = pt_ref[p] <= tr // 2
        lone = rest % 2 == 1
        tails = [(apply, lone)] if apply_first is None else [
            (apply, jnp.logical_and(lone, nsub > group)),
            (apply_first, jnp.logical_and(lone, nsub < group))]
        for ap, cond in tails:
            @pl.when(jnp.logical_and(cond, short))
            def _(ap=ap):
                run(ap, nsub - 1, 1, tr // 2)

            @pl.when(jnp.logical_and(cond, jnp.logical_not(short)))
            def _(ap=ap):
                run(ap, nsub - 1, 1)

    @pl.when(live)
    def _():
        @pl.when(c == 0)
        def _():
            for_each_subtile(first_apply, 2)

        @pl.when(jnp.logical_and(c > 0, c < last))
        def _():
            for_each_subtile(middle_apply, 4)

        @pl.when(c == last)
        def _():
            for_each_subtile(functools.partial(final_apply, reuse=True), 2,
                             functools.partial(final_apply, reuse=False))

            @pl.when(nsub >= 2)
            def _():
                y_copy(sub0 + nsub - 2, nsub % 2).wait()

            y_copy(sub0 + nsub - 1, (nsub - 1) % 2).wait()

    @pl.when(jnp.logical_and(p == pl.num_programs(0) - 1, c == last))
    def _():
        stage_ref[0] = jnp.zeros(stage_ref.shape[1:], jnp.uint32)

        def zero(s, carry):
            y_copy(s, 0).start()
            y_copy(s, 0).wait()
            return carry

        lax.fori_loop(used_ref[0], y_ref.shape[0] // (tr * slabs), zero, 0)


def _expert_ffn(xs, pass_table, used, w_gu, w_down, layer, tr, cap_sub, tc):
    rows, d = xs.shape
    fe = w_down.shape[2]
    nc = fe // tc
    assert nc >= 2
    slabs = d // (2 * LANES)
    last = nc - 1

    def chunk(c, pn, p):
        live = jnp.minimum(pn[p], 1)
        return c * live + last * (1 - live)

    grid_spec = pltpu.PrefetchScalarGridSpec(
        num_scalar_prefetch=5,
        grid=(pass_table[0].shape[0], nc),
        in_specs=[
            pl.BlockSpec(memory_space=pl.ANY),
            pl.BlockSpec((None, None, d, tc),
                         lambda p, c, pe, ps, pn, pt, us: (layer, pe[p], 0, chunk(c, pn, p))),
            pl.BlockSpec((None, None, d, tc),
                         lambda p, c, pe, ps, pn, pt, us: (layer, pe[p], 0, nc + chunk(c, pn, p))),
            pl.BlockSpec((None, None, tc, d),
                         lambda p, c, pe, ps, pn, pt, us: (layer, pe[p], chunk(c, pn, p), 0)),
        ],
        out_specs=pl.BlockSpec(memory_space=pl.ANY),
        scratch_shapes=[pltpu.VMEM((cap_sub * tr, d), BF16), pltpu.VMEM((cap_sub * tr, d), F32),
                        pltpu.VMEM((2, tr * slabs, LANES), jnp.uint32),
                        pltpu.SemaphoreType.DMA((3,))],
    )
    return pl.pallas_call(
        functools.partial(_expert_ffn_kernel, tr=tr, slabs=slabs),
        grid_spec=grid_spec,
        out_shape=jax.ShapeDtypeStruct((rows * slabs, LANES), jnp.uint32),
        compiler_params=_params(("arbitrary", "arbitrary")),
        name="moe_expert_swiglu",
    )(*pass_table, used, xs, w_gu, w_gu, w_down)


def _combine_kernel(s1_ref, s2_ref, y_ref, rw_ref, x_ref, g_ref, o_ref, b1_ref, b2_ref, sem,
                    *, slabs):
    i = pl.program_id(0)
    last = pl.num_programs(0) - 1
    tm = x_ref.shape[0]
    slot = i % GATHER_DEPTH

    def fetch(tile, wait):
        s = tile % GATHER_DEPTH
        _gather_rows(y_ref, b1_ref.at[s], sem.at[0, s], s1_ref, tile * tm, tm, slabs, wait)
        _gather_rows(y_ref, b2_ref.at[s], sem.at[1, s], s2_ref, tile * tm, tm, slabs, wait)

    @pl.when(i == 0)
    def _():
        for t in range(GATHER_DEPTH - 1):
            @pl.when(t <= last)
            def _(t=t):
                fetch(t, False)

    @pl.when(i + GATHER_DEPTH - 1 <= last)
    def _():
        fetch(i + GATHER_DEPTH - 1, False)

    fetch(i, True)

    rw = rw_ref[...]
    w1 = rw[:, 0:1]
    w2 = rw[:, 1:2]
    parts = [None] * (2 * slabs)
    ssq = jnp.zeros((tm, 1), F32)
    for f in range(slabs):
        lo1, hi1 = _unpack_pairs(b1_ref[slot, pl.ds(f, tm, stride=_pitch(slabs)), :])
        lo2, hi2 = _unpack_pairs(b2_ref[slot, pl.ds(f, tm, stride=_pitch(slabs)), :])
        for blk, a, b in ((f, lo1, lo2), (f + slabs, hi1, hi2)):
            m = w1 * a + w2 * b
            ssq = ssq + jnp.sum(m * m, axis=1, keepdims=True)
            parts[blk] = m
    inv = lax.rsqrt(ssq / (2 * slabs * LANES) + NORM_EPS)
    for blk in range(2 * slabs):
        sl = slice(blk * LANES, (blk + 1) * LANES)
        o_ref[:, sl] = x_ref[:, sl] + parts[blk] * inv * g_ref[:, sl]


def _combine(slot1, slot2, y_slabs, rw, x, g, tm):
    n, d = x.shape
    slabs = d // (2 * LANES)
    grid_spec = pltpu.PrefetchScalarGridSpec(
        num_scalar_prefetch=2,
        grid=(n // tm,),
        in_specs=[
            pl.BlockSpec(memory_space=pl.ANY),
            pl.BlockSpec((tm, LANES), lambda i, s1, s2: (i, 0)),
            pl.BlockSpec((tm, d), lambda i, s1, s2: (i, 0)),
            pl.BlockSpec((1, d), lambda i, s1, s2: (0, 0)),
        ],
        out_specs=pl.BlockSpec((tm, d), lambda i, s1, s2: (i, 0)),
        scratch_shapes=[pltpu.VMEM((GATHER_DEPTH, tm * _pitch(slabs), LANES), jnp.uint32),
                        pltpu.VMEM((GATHER_DEPTH, tm * _pitch(slabs), LANES), jnp.uint32),
                        pltpu.SemaphoreType.DMA((2, GATHER_DEPTH))],
    )
    return pl.pallas_call(
        functools.partial(_combine_kernel, slabs=slabs),
        grid_spec=grid_spec,
        out_shape=jax.ShapeDtypeStruct((n, d), F32),
        compiler_params=_params(("arbitrary",)),
        name="moe_combine",
    )(slot1, slot2, y_slabs, rw, x, g)


def _moe_ffn(x, g_in, g_out, w_router, w_gu, w_down, layer, tiles):
    n, d = x.shape
    experts = w_router.shape[1]
    assert d % (2 * LANES * SUBLANES) == 0
    tr = tiles["expert_rows"]
    cap_sub = tiles["expert_cap"] // tr
    h_slabs, ri, rw, cnt = _router(x, g_in, w_router, tiles["router_rows"])

    counts = cnt[0, :experts]
    nsub = (counts + tr - 1) // tr
    sub_end = jnp.cumsum(nsub)
    sub_start = sub_end - nsub
    slot1 = (sub_start[ri[:, 0]] * tr + ri[:, 2]).astype(jnp.int32)
    slot2 = (sub_start[ri[:, 1]] * tr + ri[:, 3]).astype(jnp.int32)
    rows = TOP_K * n + experts * tr
    n_sub = rows // tr
    token = jnp.arange(n, dtype=jnp.int32)
    tok_of_slot = jnp.zeros((rows,), jnp.int32).at[jnp.concatenate([slot1, slot2])].set(
        jnp.concatenate([token, token]), unique_indices=True)
    sub_live = (jnp.arange(n_sub, dtype=jnp.int32) < sub_end[-1]).astype(jnp.int32)

    n_pass = experts + n_sub // cap_sub
    passes = (nsub + cap_sub - 1) // cap_sub
    pass_end = jnp.cumsum(passes)
    pidx = jnp.arange(n_pass, dtype=jnp.int32)
    pe = jnp.minimum(jnp.searchsorted(pass_end, pidx, side="right"), experts - 1)
    local = pidx - (pass_end[pe] - passes[pe])
    p_live = pidx < pass_end[-1]
    pass_nsub = jnp.where(p_live, jnp.minimum(cap_sub, nsub[pe] - local * cap_sub), 0)
    pass_sub0 = jnp.where(p_live, sub_start[pe] + local * cap_sub, 0)
    last_pe = jnp.max(jnp.where(passes > 0, jnp.arange(experts), 0))
    pass_expert = jnp.where(p_live, pe, last_pe)
    pass_tokens = jnp.minimum(cap_sub * tr, counts[pe] - local * cap_sub * tr)
    pass_tail = jnp.where(p_live, pass_tokens - (pass_nsub - 1) * tr, 0)
    pass_table = tuple(a.astype(jnp.int32) for a in (pass_expert, pass_sub0, pass_nsub, pass_tail))

    xs = _dispatch(tok_of_slot, sub_live, h_slabs, d, tr)
    y_slabs = _expert_ffn(xs, pass_table, sub_end[-1:].astype(jnp.int32), w_gu, w_down, layer, tr,
                          cap_sub, tiles["expert_chunk"])
    return _combine(slot1, slot2, y_slabs, rw, x, g_out, tiles["combine_rows"])


def _tiles(n, seq, d, ffn, expert_ffn):
    pick = lambda pref, dim: min(pref, dim)
    return {
        "conv_rows": pick(256, seq), "conv_cols": pick(512, d),
        "ln_rows": pick(512, n),
        "ffn_rows": pick(512, n), "ffn_chunk": pick(512, ffn),
        "proj_rows": pick(256, seq),
        "attn_q": pick(256, seq), "attn_k": pick(256, seq),
        "out_rows": pick(512, n),
        "router_rows": pick(512, n),
        "expert_rows": pick(256, n), "expert_cap": pick(2304, TOP_K * n),
        "expert_chunk": pick(256, expert_ffn),
        "combine_rows": pick(256, n),
    }


def _rope_tables(seq):
    pos = jnp.arange(seq, dtype=F32)
    inv_freq = ROPE_THETA ** (-jnp.arange(0, QK_ROPE_DIM, 2, dtype=F32) / QK_ROPE_DIM)
    ang = pos[:, None] * inv_freq[None, :]
    cos, sin = jnp.cos(ang), jnp.sin(ang)
    zeros = jnp.zeros((seq, LANES - QK_ROPE_DIM), F32)
    return (jnp.concatenate([cos, cos, zeros], axis=1),
            jnp.concatenate([-sin, sin, zeros], axis=1))


def _swap_halves(w):
    half = w.shape[-1] // 2
    return jnp.concatenate([w[..., half:], w[..., :half]], axis=-1)


def kernel(x, norm_g, conv_w_pw1, conv_b_pw1, conv_w_dw, conv_b_dw, conv_ln_g, conv_ln_b,
           conv_w_pw2, ffn_w_gu, ffn_w_down, moe_w_router, moe_w_gu, moe_w_down,
           kv_in_g, kv_w_a, kv_latent_g, kv_w_b, attn_w_q_a, attn_q_latent_g,
           attn_w_q_b, attn_w_o):
    batch, seq, d = x.shape
    n = batch * seq
    depth = norm_g.shape[0]
    n_conv = depth // 2
    heads = d // 128
    kv_rank = kv_latent_g.shape[0]
    tiles = _tiles(n, seq, d, ffn_w_down.shape[1], moe_w_down.shape[2])
    cos2, sin2 = _rope_tables(seq)
    vec = lambda v: v.reshape(1, -1)

    xs = x.reshape(n, d)
    k = v = None
    for i in range(depth):
        if i < n_conv:
            c = _pw1_dwconv(xs, vec(norm_g[i, 0]), conv_w_pw1[i].astype(BF16), vec(conv_b_pw1[i]),
                            conv_w_dw[i], vec(conv_b_dw[i]), seq, tiles["conv_rows"],
                            tiles["conv_cols"])
            xs = _ln_pw2(c, xs, vec(conv_ln_g[i]), vec(conv_ln_b[i]), conv_w_pw2[i].astype(BF16),
                         vec(norm_g[i, 1]), tiles["ln_rows"])
        else:
            j = i - n_conv
            q_rank = attn_w_q_a.shape[2]
            wqb = attn_w_q_b[j].reshape(q_rank, heads, QK_NOPE_DIM + QK_ROPE_DIM)
            rope_w = wqb[:, :, QK_NOPE_DIM:]
            wqb_ext = jnp.concatenate([wqb, _swap_halves(rope_w)], axis=2)
            wqb_ext = wqb_ext.reshape(q_rank, heads * 256).astype(BF16)
            q_args = (attn_w_q_a[j].astype(BF16), vec(attn_q_latent_g[j]), wqb_ext, cos2, sin2,
                      seq, tiles["proj_rows"])
            if j == 0:
                rope_w = kv_w_a[:, kv_rank:]
                wa_ext = jnp.concatenate([kv_w_a, _swap_halves(rope_w)], axis=1).astype(BF16)
                wb = kv_w_b.reshape(kv_rank, heads, QK_NOPE_DIM + V_DIM)
                wb_split = jnp.concatenate(
                    [wb[:, :, :QK_NOPE_DIM].reshape(kv_rank, heads * QK_NOPE_DIM),
                     wb[:, :, QK_NOPE_DIM:].reshape(kv_rank, heads * V_DIM)], axis=1).astype(BF16)
                k, v, q = _qkv_proj(xs, vec(kv_in_g), vec(norm_g[i, 0]), wa_ext, vec(kv_latent_g),
                                    wb_split, *q_args)
            else:
                q = _q_proj(xs, vec(norm_g[i, 0]), *q_args)
            o = _attention(q, k, v, batch, seq, tiles["attn_q"], tiles["attn_k"])
            xs = _attn_out(o, xs, attn_w_o[j].astype(BF16), vec(norm_g[i, 1]), tiles["out_rows"])
        if i % 2 == 0:
            xs = _dense_ffn(xs, vec(norm_g[i, 2]), vec(norm_g[i, 3]), ffn_w_gu[i // 2].astype(BF16),
                            ffn_w_down[i // 2].astype(BF16), tiles["ffn_rows"], tiles["ffn_chunk"])
        else:
            xs = _moe_ffn(xs, vec(norm_g[i, 2]), vec(norm_g[i, 3]), moe_w_router[i // 2],
                          moe_w_gu, moe_w_down, i // 2, tiles)
    return xs.reshape(batch, seq, d)
```
